```python
import math
import jax
import jax.numpy as jnp
from jax import lax
import numpy as np

D_MODEL = 1024
BATCH = 8
SEQ = 2048
DEPTH = 4

GRID_W = 64
CTX_LEN = 256
N_EVEN = (DEPTH + 1) // 2
N_ODD = DEPTH // 2
N_MOD = 9
DEEPNORM_ALPHA = (2.0 * DEPTH) ** 0.25
DEEPNORM_BETA = (8.0 * DEPTH) ** -0.25
LN_EPS = 1e-5
RMS_EPS = 1e-6
MACARON_WEIGHT = 0.5
FFN_HIDDEN = ((8 * D_MODEL // 3 + 255) // 256) * 256

HYENA_WIDTH = D_MODEL // 2
HYENA_ORDER = 2
HYENA_FILTERS = HYENA_ORDER - 1
HYENA_PROJ = (HYENA_ORDER + 1) * HYENA_WIDTH
HYENA_BANDS = 16
HYENA_EMB = 1 + 2 * HYENA_BANDS
HYENA_FILTER_HIDDEN = 64
HYENA_FAST_DECAY = 0.3
HYENA_SLOW_DECAY = 1.5
HYENA_TARGET = 1e-2

GLA_WIDTH = D_MODEL - HYENA_WIDTH
GLA_HEADS = 4
GLA_DV = GLA_WIDTH // GLA_HEADS
GLA_DK = GLA_DV // 2
GLA_QK = GLA_HEADS * GLA_DK
GLA_RANK = 16
GLA_TAU = 16.0
GLA_CHUNK = 64
EVEN_SIZES = (HYENA_PROJ, GLA_QK, GLA_QK, GLA_WIDTH, GLA_WIDTH, GLA_RANK, GLA_RANK)
EVEN_IN = HYENA_PROJ + 2 * GLA_QK + 2 * GLA_WIDTH + 2 * GLA_RANK

ATTN_HEAD_DIM = 128
ATTN_Q_HEADS = D_MODEL // ATTN_HEAD_DIM
ATTN_KV_HEADS = 2
ATTN_GROUP = ATTN_Q_HEADS // ATTN_KV_HEADS
ATTN_BLOCK = 128
ATTN_Q_WIDTH = ATTN_Q_HEADS * ATTN_HEAD_DIM
ATTN_KV_WIDTH = ATTN_KV_HEADS * ATTN_HEAD_DIM
ODD_IN = ATTN_Q_WIDTH + 2 * ATTN_KV_WIDTH
ROPE_THETA = 10000.0
ROPE_AXIS_PAIRS = ATTN_HEAD_DIM // 4

kernel_name = 'hybrid_hyena_gla_gqa_diffusion_trunk'


def layer_norm(x, g, b):
    xf = x.astype(jnp.float32)
    mu = jnp.mean(xf, -1, keepdims=True)
    var = jnp.mean(jnp.square(xf - mu), -1, keepdims=True)
    return ((xf - mu) * lax.rsqrt(var + LN_EPS) * g + b).astype(x.dtype)


def rms_norm(x, g):
    xf = x.astype(jnp.float32)
    return (xf * lax.rsqrt(jnp.mean(xf * xf, -1, keepdims=True) + RMS_EPS) * g).astype(x.dtype)


def mod_input(z, mm, slot):
    return z * (1.0 + mm[:, 3 * slot + 1]) + mm[:, 3 * slot]


def gated_post_norm(z, y, mm, slot, g, b):
    return layer_norm(DEEPNORM_ALPHA * z + mm[:, 3 * slot + 2] * y, g, b)


def swiglu(h, w1, w2):
    gate, up = jnp.split(h @ w1, 2, axis=-1)
    return (jax.nn.silu(gate) * up) @ w2


def ffn_sublayer(z, mm, slot, w1, w2, g, b):
    y = MACARON_WEIGHT * swiglu(mod_input(z, mm, slot), w1, w2)
    return gated_post_norm(z, y, mm, slot, g, b)


def split_sizes(t, sizes):
    idx, acc = [], 0
    for s in sizes[:-1]:
        acc += s
        idx.append(acc)
    return jnp.split(t, idx, axis=-1)


def short_conv3(u, w, b):
    up = jnp.pad(u, ((0, 0), (1, 1), (0, 0)))
    return up[:, :-2] * w[0] + up[:, 1:-1] * w[1] + up[:, 2:] * w[2] + b


def hyena_filter_spectrum(length, w1, b1, fr1, w2, b2, fr2, w3, b3):
    f32 = jnp.float32
    n = jnp.arange(length, dtype=f32)[:, None]
    t = jnp.linspace(0.0, 1.0, length, dtype=f32)[:, None]
    bands = jnp.linspace(1e-4, HYENA_BANDS - 1, HYENA_BANDS, dtype=f32)[None, :]
    ang = 2.0 * math.pi * n * bands / length
    feats = jnp.concatenate([t, jnp.cos(ang), -jnp.sin(ang)], axis=-1)
    h = jnp.sin(fr1.astype(f32) * (feats @ w1.astype(f32) + b1.astype(f32)))
    h = jnp.sin(fr2.astype(f32) * (h @ w2.astype(f32) + b2.astype(f32)))
    h = (h @ w3.astype(f32) + b3.astype(f32)).reshape(length, HYENA_FILTERS, 2, HYENA_WIDTH)
    max_decay = math.log(HYENA_TARGET) / HYENA_FAST_DECAY
    min_decay = math.log(HYENA_TARGET) / HYENA_SLOW_DECAY
    deltas = jnp.abs(jnp.linspace(min_decay, max_decay, HYENA_WIDTH, dtype=f32))
    h = h * jnp.exp(-t[:, :, None, None] * deltas)
    h_fwd, h_bwd = h[:, :, 0], h[:, :, 1]
    k = jnp.concatenate([h_fwd, jnp.zeros_like(h_fwd[:1]), jnp.flip(h_bwd[1:], axis=0)], axis=0)
    k = k * lax.rsqrt(jnp.sum(jnp.square(k), axis=0, keepdims=True) + 1e-6)
    return jnp.fft.rfft(k, axis=0)


def fft_long_conv(z, k_hat, skip):
    L = z.shape[1]
    z_hat = jnp.fft.rfft(z.astype(jnp.float32), n=2 * L, axis=1)
    y = jnp.fft.irfft(z_hat * k_hat, n=2 * L, axis=1)[:, :L]
    return (y + z.astype(jnp.float32) * skip).astype(z.dtype)


def hyena_mix(u, k_hat, conv_w, conv_b, skip):
    parts = jnp.split(short_conv3(u, conv_w, conv_b), HYENA_ORDER + 1, axis=-1)
    z = parts[0]
    for n in range(HYENA_FILTERS):
        z = fft_long_conv(z * parts[n + 1], k_hat[:, n], skip[n])
    return z * parts[HYENA_ORDER]


def gla_chunked(q, k, v, log_a, s0):
    f32 = jnp.float32
    Bn, L, H, _ = q.shape
    n = L // GLA_CHUNK

    def chunks(t):
        return t.reshape(Bn, n, GLA_CHUNK, H, t.shape[-1]).astype(f32)

    qc, kc, vc, gc = chunks(q), chunks(k), chunks(v), chunks(log_a)
    b = jnp.cumsum(gc, axis=2)
    b_last = b[:, :, -1:]
    q_in = qc * jnp.exp(b)
    k_in = kc * jnp.exp(-b)
    k_st = kc * jnp.exp(b_last - b)
    mask = jnp.tril(jnp.ones((GLA_CHUNK, GLA_CHUNK), dtype=bool))
    att = jnp.where(mask, jnp.einsum('bnchd,bnshd->bnhcs', q_in, k_in), 0.0)
    o_intra = jnp.einsum('bnhcs,bnshe->bnche', att, vc)
    ds = jnp.einsum('bnchd,bnche->nbhde', k_st, vc)
    decay = jnp.moveaxis(jnp.exp(b_last[:, :, 0]), 1, 0)

    def step(s, inp):
        dec, d = inp
        return dec[..., None] * s + d, s

    s_final, s_before = lax.scan(step, s0.astype(f32), (decay, ds))
    o_inter = jnp.einsum('bnchd,nbhde->bnche', q_in, s_before)
    o = (o_intra + o_inter).reshape(Bn, L, H, v.shape[-1])
    return o.astype(v.dtype), s_final


def gla_reverse(q, k, v, log_a, s0):
    o, s = gla_chunked(jnp.flip(q, 1), jnp.flip(k, 1), jnp.flip(v, 1), jnp.flip(log_a, 1), s0)
    return jnp.flip(o, 1), s


def even_mixer(a, ac, w_in, w_out, conv_w, conv_b, f_w1, f_b1, f_fr1, f_w2, f_b2, f_fr2, f_w3, f_b3,
               skip, gate_up, gate_b, norm_g, need_ctx):
    def project(z):
        Bn, L = z.shape[:2]
        hy, q, k, v, g, af, ab = split_sizes(z @ w_in, EVEN_SIZES)

        def heads(t, d):
            return t.reshape(Bn, L, GLA_HEADS, d)

        def log_gate(lo, j):
            return heads(jax.nn.log_sigmoid((lo @ gate_up[j] + gate_b[j]).astype(jnp.float32)) / GLA_TAU, GLA_DK)

        return (hy, heads(q, GLA_DK) * GLA_DK ** -0.5, heads(k, GLA_DK), heads(v, GLA_DV), g,
                log_gate(af, 0), log_gate(ab, 1))

    def hyena(hy):
        k_hat = hyena_filter_spectrum(hy.shape[1], f_w1, f_b1, f_fr1, f_w2, f_b2, f_fr2, f_w3, f_b3)
        return hyena_mix(hy, k_hat, conv_w, conv_b, skip)

    def gla_merge(o, g):
        Bn, L = g.shape[:2]
        return rms_norm(o, norm_g).reshape(Bn, L, GLA_WIDTH) * jax.nn.silu(g)

    hy, q, k, v, g, lf, lb = project(a)
    hyc, qc, kc, vc, gc, lfc, lbc = project(ac)
    s0 = jnp.zeros((ac.shape[0], GLA_HEADS, GLA_DK, GLA_DV), jnp.float32)
    oc_f, sc_f = gla_chunked(qc, kc, vc, lfc, s0)
    oc_b, sc_b = gla_reverse(qc, kc, vc, lbc, s0)
    o_f, _ = gla_chunked(q, k, v, lf, sc_f)
    o_b, _ = gla_reverse(q, k, v, lb, sc_b)
    y = jnp.concatenate([hyena(hy), gla_merge(o_f + o_b, g)], axis=-1) @ w_out
    yc = jnp.concatenate([hyena(hyc), gla_merge(oc_f + oc_b, gc)], axis=-1) @ w_out if need_ctx else None
    return y, yc


def rope_axis(x, ang):
    c = jnp.cos(ang)[None, :, None, :]
    s = jnp.sin(ang)[None, :, None, :]
    x1, x2 = jnp.split(x.astype(jnp.float32), 2, axis=-1)
    return jnp.concatenate([x1 * c - x2 * s, x2 * c + x1 * s], axis=-1)


def rope_2d(x, ang_r, ang_c):
    xr, xc = jnp.split(x, 2, axis=-1)
    return jnp.concatenate([rope_axis(xr, ang_r), rope_axis(xc, ang_c)], axis=-1).astype(x.dtype)


def gqa_attend(q, k, v):
    Bn, Lq = q.shape[:2]
    qg = q.reshape(Bn, Lq, ATTN_KV_HEADS, ATTN_GROUP, ATTN_HEAD_DIM)
    s = jnp.einsum('bqkgd,bskd->bkgqs', qg, k, preferred_element_type=jnp.float32) * ATTN_HEAD_DIM ** -0.5
    p = jax.nn.softmax(s, axis=-1).astype(v.dtype)
    o = jnp.einsum('bkgqs,bskd->bqkgd', p, v)
    return o.reshape(Bn, Lq, ATTN_Q_WIDTH)


def odd_mixer(a, ac, w_in, w_out, q_norm, k_norm, ang_r, ang_c, need_ctx):
    def project(z):
        Bn, L = z.shape[:2]
        q, k, v = jnp.split(z @ w_in, [ATTN_Q_WIDTH, ATTN_Q_WIDTH + ATTN_KV_WIDTH], axis=-1)
        q = rms_norm(q.reshape(Bn, L, ATTN_Q_HEADS, ATTN_HEAD_DIM), q_norm)
        k = rms_norm(k.reshape(Bn, L, ATTN_KV_HEADS, ATTN_HEAD_DIM), k_norm)
        return q, k, v.reshape(Bn, L, ATTN_KV_HEADS, ATTN_HEAD_DIM)

    q, k, v = project(a)
    q, k = rope_2d(q, ang_r, ang_c), rope_2d(k, ang_r, ang_c)
    qc, kc, vc = project(ac)
    k_all = jnp.concatenate([k, kc], axis=1)
    v_all = jnp.concatenate([v, vc], axis=1)
    Bn, S = a.shape[:2]
    q_blocks = q.reshape(Bn, S // ATTN_BLOCK, ATTN_BLOCK, ATTN_Q_HEADS, ATTN_HEAD_DIM).swapaxes(0, 1)
    o = lax.map(lambda qb: gqa_attend(qb, k_all, v_all), q_blocks)
    y = o.swapaxes(0, 1).reshape(Bn, S, ATTN_Q_WIDTH) @ w_out
    yc = gqa_attend(qc, kc, vc) @ w_out if need_ctx else None
    return y, yc


def setup_inputs(seed: int = 0) -> dict:
    key = jax.random.key(seed)
    ks = iter(jax.random.split(key, 48))
    f32 = jnp.float32
    D = D_MODEL

    def nrm(shape, scale):
        return scale * jax.random.normal(next(ks), shape, f32)

    return {
        'x': nrm((BATCH, SEQ, D), 1.0),
        'c': nrm((BATCH, D), 1.0),
        'ctx': nrm((BATCH, CTX_LEN, D), 1.0),
        'c_ctx': nrm((D,), 1.0),
        'ada_w': nrm((DEPTH, D, N_MOD * D), D ** -0.5),
        'ada_b': nrm((DEPTH, N_MOD * D), 0.02),
        'ln_g': 1.0 + nrm((DEPTH, 3, D), 0.02),
        'ln_b': nrm((DEPTH, 3, D), 0.02),
        'ffn_w1': nrm((DEPTH, 2, D, 2 * FFN_HIDDEN), D ** -0.5),
        'ffn_w2': nrm((DEPTH, 2, FFN_HIDDEN, D), DEEPNORM_BETA * FFN_HIDDEN ** -0.5),
        'even_w_in': nrm((N_EVEN, D, EVEN_IN), D ** -0.5),
        'even_w_out': nrm((N_EVEN, D, D), DEEPNORM_BETA * D ** -0.5),
        'hyena_conv_w': nrm((N_EVEN, 3, HYENA_PROJ), 3 ** -0.5),
        'hyena_conv_b': nrm((N_EVEN, HYENA_PROJ), 0.02),
        'hyena_f_w1': nrm((N_EVEN, HYENA_EMB, HYENA_FILTER_HIDDEN), HYENA_EMB ** -0.5),
        'hyena_f_b1': nrm((N_EVEN, HYENA_FILTER_HIDDEN), 0.1),
        'hyena_f_fr1': 1.0 + nrm((N_EVEN, HYENA_FILTER_HIDDEN), 0.1),
        'hyena_f_w2': nrm((N_EVEN, HYENA_FILTER_HIDDEN, HYENA_FILTER_HIDDEN), HYENA_FILTER_HIDDEN ** -0.5),
        'hyena_f_b2': nrm((N_EVEN, HYENA_FILTER_HIDDEN), 0.1),
        'hyena_f_fr2': 1.0 + nrm((N_EVEN, HYENA_FILTER_HIDDEN), 0.1),
        'hyena_f_w3': nrm((N_EVEN, HYENA_FILTER_HIDDEN, HYENA_FILTERS * 2 * HYENA_WIDTH), HYENA_FILTER_HIDDEN ** -0.5),
        'hyena_f_b3': nrm((N_EVEN, HYENA_FILTERS * 2 * HYENA_WIDTH), 0.02),
        'hyena_skip': nrm((N_EVEN, HYENA_FILTERS, HYENA_WIDTH), 0.1),
        'gla_gate_up': nrm((N_EVEN, 2, GLA_RANK, GLA_QK), GLA_RANK ** -0.5),
        'gla_gate_b': nrm((N_EVEN, 2, GLA_QK), 0.1),
        'gla_norm_g': 1.0 + nrm((N_EVEN, GLA_DV), 0.02),
        'attn_w_in': nrm((N_ODD, D, ODD_IN), D ** -0.5),
        'attn_w_out': nrm((N_ODD, D, D), DEEPNORM_BETA * D ** -0.5),
        'attn_q_norm': 1.0 + nrm((N_ODD, ATTN_HEAD_DIM), 0.02),
        'attn_k_norm': 1.0 + nrm((N_ODD, ATTN_HEAD_DIM), 0.02),
    }


def reference(x, c, ctx, c_ctx, ada_w, ada_b, ln_g, ln_b, ffn_w1, ffn_w2, even_w_in, even_w_out,
              hyena_conv_w, hyena_conv_b, hyena_f_w1, hyena_f_b1, hyena_f_fr1, hyena_f_w2, hyena_f_b2,
              hyena_f_fr2, hyena_f_w3, hyena_f_b3, hyena_skip, gla_gate_up, gla_gate_b, gla_norm_g,
              attn_w_in, attn_w_out, attn_q_norm, attn_k_norm):
    f32 = jnp.float32
    Bn, S = x.shape[:2]
    rows = S // GRID_W
    row = jnp.repeat(jnp.arange(rows, dtype=f32), GRID_W)
    col = jnp.tile(jnp.arange(GRID_W, dtype=f32), rows)
    inv_freq = ROPE_THETA ** (-jnp.arange(ROPE_AXIS_PAIRS, dtype=f32) / ROPE_AXIS_PAIRS)
    ang_r = row[:, None] * inv_freq
    ang_c = col[:, None] * inv_freq
    sc = jax.nn.silu(c)
    sc_ctx = jax.nn.silu(c_ctx)
    h, hc = x, ctx
    for i in range(DEPTH):
        need_ctx = i < DEPTH - 1
        m = (sc @ ada_w[i] + ada_b[i]).reshape(Bn, N_MOD, 1, D_MODEL)
        mc = (sc_ctx @ ada_w[i] + ada_b[i]).reshape(1, N_MOD, 1, D_MODEL)
        h = ffn_sublayer(h, m, 0, ffn_w1[i, 0], ffn_w2[i, 0], ln_g[i, 0], ln_b[i, 0])
        hc = ffn_sublayer(hc, mc, 0, ffn_w1[i, 0], ffn_w2[i, 0], ln_g[i, 0], ln_b[i, 0])
        a, ac = mod_input(h, m, 1), mod_input(hc, mc, 1)
        if i % 2 == 0:
            e = i // 2
            y, yc = even_mixer(a, ac, even_w_in[e], even_w_out[e], hyena_conv_w[e], hyena_conv_b[e],
                               hyena_f_w1[e], hyena_f_b1[e], hyena_f_fr1[e], hyena_f_w2[e], hyena_f_b2[e],
                               hyena_f_fr2[e], hyena_f_w3[e], hyena_f_b3[e], hyena_skip[e],
                               gla_gate_up[e], gla_gate_b[e], gla_norm_g[e], need_ctx)
        else:
            o = i // 2
            y, yc = odd_mixer(a, ac, attn_w_in[o], attn_w_out[o], attn_q_norm[o], attn_k_norm[o],
                              ang_r, ang_c, need_ctx)
        h = gated_post_norm(h, y, m, 1, ln_g[i, 1], ln_b[i, 1])
        h = ffn_sublayer(h, m, 2, ffn_w1[i, 1], ffn_w2[i, 1], ln_g[i, 2], ln_b[i, 2])
        if need_ctx:
            hc = gated_post_norm(hc, yc, mc, 1, ln_g[i, 1], ln_b[i, 1])
            hc = ffn_sublayer(hc, mc, 2, ffn_w1[i, 1], ffn_w2[i, 1], ln_g[i, 2], ln_b[i, 2])
    return h
```

```python
import functools
import math

import jax
import jax.numpy as jnp
import numpy as np
from jax import lax
from jax.experimental import pallas as pl
from jax.experimental.pallas import tpu as pltpu

f32 = jnp.float32
bf16 = jnp.bfloat16

D_MODEL = 1024
BATCH = 8
SEQ = 2048
DEPTH = 4
GRID_W = 64
CTX_LEN = 256
N_MOD = 9
DEEPNORM_ALPHA = (2.0 * DEPTH) ** 0.25
LN_EPS = 1e-5
RMS_EPS = 1e-6
MACARON_WEIGHT = 0.5
FFN_HIDDEN = 2816

HYENA_WIDTH = 512
HYENA_PROJ = 3 * HYENA_WIDTH
HYENA_BANDS = 16
HYENA_EMB = 1 + 2 * HYENA_BANDS
HYENA_FILTER_HIDDEN = 64
HYENA_FAST_DECAY = 0.3
HYENA_SLOW_DECAY = 1.5
HYENA_TARGET = 1e-2

GLA_WIDTH = 512
GLA_HEADS = 4
GLA_DV = 128
GLA_DK = 64
GLA_QK = GLA_HEADS * GLA_DK
GLA_RANK = 16
GLA_TAU = 16.0
GLA_CHUNK = 64

ATTN_HEAD_DIM = 128
ATTN_Q_HEADS = 8
ATTN_KV_HEADS = 2
ATTN_GROUP = 4
ATTN_Q_WIDTH = 1024
ATTN_KV_WIDTH = 256
ROPE_THETA = 10000.0
ROPE_AXIS_PAIRS = 32

assert BATCH * CTX_LEN == SEQ
NB = BATCH + 1

LANE = 128
MOD_ROWS = 16
EMB_PAD = 128
VMEM_LIMIT = 56 * 1024 * 1024
TOKEN_TILE = 512

EVEN_Q = HYENA_PROJ
EVEN_K = EVEN_Q + GLA_HEADS * LANE
EVEN_V = EVEN_K + GLA_HEADS * LANE
EVEN_G = EVEN_V + GLA_WIDTH
EVEN_LO = EVEN_G + GLA_WIDTH
EVEN_PAD = EVEN_LO + LANE
HY_CBLK = 256
HY_FT = 512
V_EXT = 2 * LANE


def _cparams(*sem):
    return pltpu.CompilerParams(dimension_semantics=sem, vmem_limit_bytes=VMEM_LIMIT)


def _resident(shape, index=None):
    index = tuple(index or ())
    block = (None,) * len(index) + tuple(shape[len(index):])
    tail = (0,) * (len(shape) - len(index))
    return pl.BlockSpec(block, lambda *_: index + tail, pipeline_mode=pl.Buffered(1))


def _dot(a, b):
    return jnp.dot(a, b, preferred_element_type=f32)


def _dot_nt(a, b):
    return lax.dot_general(a, b, (((1,), (1,)), ((), ())), preferred_element_type=f32)


def _dot_tn(a, b):
    return lax.dot_general(a, b, (((0,), (0,)), ((), ())), preferred_element_type=f32)


def _split3(x):
    hi = x.astype(bf16)
    r1 = x - hi.astype(f32)
    mid = r1.astype(bf16)
    lo = (r1 - mid.astype(f32)).astype(bf16)
    return hi, mid, lo


def _dot_f32(a, b):
    a0, a1, a2 = _split3(a)
    b0, b1, b2 = _split3(b)
    return (_dot(a0, b0) + (_dot(a0, b1) + _dot(a1, b0))
            + (_dot(a0, b2) + _dot(a1, b1) + _dot(a2, b0)))


def _layer_norm(x, g, b):
    mu = jnp.mean(x, -1, keepdims=True)
    xc = x - mu
    var = jnp.mean(xc * xc, -1, keepdims=True)
    return xc * lax.rsqrt(var + LN_EPS) * g + b


def _mod_rows(mod_ref, slot):
    return (mod_ref[3 * slot:3 * slot + 1, :], mod_ref[3 * slot + 1:3 * slot + 2, :],
            mod_ref[3 * slot + 2:3 * slot + 3, :])


def _mod_spec(layer):
    return pl.BlockSpec((None, None, N_MOD, D_MODEL), lambda b, *_: (layer, b, 0, 0))


def _tok_spec(width):
    return pl.BlockSpec((None, TOKEN_TILE, width), lambda b, t: (b, t, 0))


def _ln_specs(layer, slot):
    return [_resident((DEPTH, 3, 1, D_MODEL), (layer, slot))] * 2


def _adaln_kernel(c_ref, w_ref, b_ref, o_ref):
    c = c_ref[...]
    sc = (c * jax.nn.sigmoid(c)).astype(bf16)
    o_ref[...] = _dot(sc, w_ref[...].astype(bf16)) + b_ref[...]


def _adaln(c_all, ada_w, ada_b):
    return pl.pallas_call(
        _adaln_kernel,
        grid=(DEPTH, N_MOD),
        in_specs=[
            pl.BlockSpec((MOD_ROWS, D_MODEL), lambda l, j: (0, 0)),
            pl.BlockSpec((None, D_MODEL, D_MODEL), lambda l, j: (l, 0, j)),
            pl.BlockSpec((None, 1, D_MODEL), lambda l, j: (l, 0, j)),
        ],
        out_specs=pl.BlockSpec((None, MOD_ROWS, D_MODEL), lambda l, j: (l, 0, j)),
        out_shape=jax.ShapeDtypeStruct((DEPTH, MOD_ROWS, N_MOD * D_MODEL), f32),
        compiler_params=_cparams("parallel", "parallel"),
        name="adaln",
    )(c_all, ada_w, ada_b.reshape(DEPTH, 1, N_MOD * D_MODEL))


FFN_TH = 256


def _ffn_kernel(slot, h_ref, mod_ref, w1_ref, w2_ref, g_ref, b_ref, o_ref):
    shift, scale, gate = _mod_rows(mod_ref, slot)
    h = h_ref[...]
    a = (h * (1.0 + scale) + shift).astype(bf16)
    acc = None
    for lo in range(0, FFN_HIDDEN, FFN_TH):
        gt = _dot(a, w1_ref[:, lo:lo + FFN_TH])
        up = _dot(a, w1_ref[:, FFN_HIDDEN + lo:FFN_HIDDEN + lo + FFN_TH])
        act = (gt * jax.nn.sigmoid(gt) * up).astype(bf16)
        part = _dot(act, w2_ref[lo:lo + FFN_TH, :])
        acc = part if acc is None else acc + part
    o_ref[...] = _layer_norm(DEEPNORM_ALPHA * h + gate * (MACARON_WEIGHT * acc), g_ref[...], b_ref[...])


def _ffn(h, mods, layer, slot, which, w1, w2, ln_g, ln_b, n_slabs):
    return pl.pallas_call(
        functools.partial(_ffn_kernel, slot),
        grid=(n_slabs, SEQ // TOKEN_TILE),
        in_specs=[_tok_spec(D_MODEL), _mod_spec(layer), _resident(w1.shape, (layer, which)),
                  _resident(w2.shape, (layer, which)), *_ln_specs(layer, slot)],
        out_specs=_tok_spec(D_MODEL),
        out_shape=jax.ShapeDtypeStruct((n_slabs, SEQ, D_MODEL), f32),
        compiler_params=_cparams("parallel", "parallel"),
        name="ffn",
    )(h, mods, w1, w2, ln_g, ln_b)


def _proj_kernel(tn, h_ref, mod_ref, w_ref, o_ref):
    shift, scale, _ = _mod_rows(mod_ref, 1)
    a = (h_ref[...] * (1.0 + scale) + shift).astype(bf16)
    n = w_ref.shape[1]
    for j in range(0, n, tn):
        w = min(tn, n - j)
        o_ref[:, j:j + w] = _dot(a, w_ref[:, j:j + w])


def _proj(h, mods, layer, w):
    n = w.shape[1]
    return pl.pallas_call(
        functools.partial(_proj_kernel, 512),
        grid=(NB, SEQ // TOKEN_TILE),
        in_specs=[_tok_spec(D_MODEL), _mod_spec(layer), _resident(w.shape)],
        out_specs=_tok_spec(n),
        out_shape=jax.ShapeDtypeStruct((NB, SEQ, n), f32),
        compiler_params=_cparams("parallel", "parallel"),
        name="even_proj",
    )(h, mods, w)


ATTN_PROJ_TILE = 256
Q_PRESCALE = ATTN_HEAD_DIM ** -0.5 * math.log2(math.e)


def _attn_proj_kernel(h_ref, mod_ref, w_ref, qn_ref, kn_ref, cos_ref, sin_ref, q_ref, k_ref, v_ref):
    shift, scale, _ = _mod_rows(mod_ref, 1)
    a = (h_ref[...] * (1.0 + scale) + shift).astype(bf16)
    is_ctx = pl.program_id(0) == BATCH
    cos = jnp.where(is_ctx, 1.0, cos_ref[...])
    sin = jnp.where(is_ctx, 0.0, sin_ref[...])

    def norm_rope(x, gain):
        x = x * lax.rsqrt(jnp.mean(x * x, -1, keepdims=True) + RMS_EPS) * gain
        return x * cos + pltpu.roll(x, ATTN_HEAD_DIM // 2, 1) * sin

    for half in range(2):
        x4 = _dot(a, w_ref[:, half * 512:(half + 1) * 512])
        for hh in range(4):
            hd = half * 4 + hh
            q_ref[:, hd * LANE:(hd + 1) * LANE] = (
                norm_rope(x4[:, hh * LANE:(hh + 1) * LANE], qn_ref[...]) * Q_PRESCALE).astype(bf16)
    kv = _dot(a, w_ref[:, ATTN_Q_WIDTH:ATTN_Q_WIDTH + 2 * ATTN_KV_WIDTH])
    ones = jnp.ones((a.shape[0], LANE), bf16)
    for hd in range(ATTN_KV_HEADS):
        k_ref[:, hd * LANE:(hd + 1) * LANE] = norm_rope(kv[:, hd * LANE:(hd + 1) * LANE], kn_ref[...]).astype(bf16)
        v_ref[:, hd * V_EXT:hd * V_EXT + LANE] = kv[:, ATTN_KV_WIDTH + hd * LANE:ATTN_KV_WIDTH + (hd + 1) * LANE].astype(bf16)
        v_ref[:, hd * V_EXT + LANE:(hd + 1) * V_EXT] = ones


def _attn_proj(h, mods, layer, w, qn, kn, cos, sin):
    tm = ATTN_PROJ_TILE
    tok = lambda width: pl.BlockSpec((None, tm, width), lambda b, t: (b, t, 0))
    rope = pl.BlockSpec((tm, LANE), lambda b, t: (t, 0))
    return pl.pallas_call(
        _attn_proj_kernel,
        grid=(NB, SEQ // tm),
        in_specs=[tok(D_MODEL), _mod_spec(layer), _resident(w.shape), _resident((1, LANE)), _resident((1, LANE)),
                  rope, rope],
        out_specs=[tok(ATTN_Q_WIDTH), tok(ATTN_KV_WIDTH), tok(ATTN_KV_HEADS * V_EXT)],
        out_shape=[jax.ShapeDtypeStruct((NB, SEQ, ATTN_Q_WIDTH), bf16),
                   jax.ShapeDtypeStruct((NB, SEQ, ATTN_KV_WIDTH), bf16),
                   jax.ShapeDtypeStruct((NB, SEQ, ATTN_KV_HEADS * V_EXT), bf16)],
        compiler_params=_cparams("parallel", "parallel"),
        name="attn_proj",
    )(h, mods, w, qn, kn, cos, sin)


def _attn_kernel(has_lat, q_ref, kc_ref, vc_ref, *rest):
    if has_lat:
        k_ref, v_ref, o_ref = rest
    else:
        (o_ref,) = rest
    for g in range(ATTN_KV_HEADS):
        cols = slice(g * LANE, (g + 1) * LANE)
        vcols = slice(g * V_EXT, (g + 1) * V_EXT)
        for hh in range(ATTN_GROUP):
            hcols = slice((g * ATTN_GROUP + hh) * LANE, (g * ATTN_GROUP + hh + 1) * LANE)
            q = q_ref[:, hcols]
            sc = _dot_nt(q, kc_ref[:, cols])
            m = jnp.max(sc, -1, keepdims=True)
            if has_lat:
                s = _dot_nt(q, k_ref[:, cols])
                m = jnp.maximum(m, jnp.max(s, -1, keepdims=True))
                ov = _dot(jnp.exp2(s - m).astype(bf16), v_ref[:, vcols])
                ov = ov + _dot(jnp.exp2(sc - m).astype(bf16), vc_ref[:, vcols])
            else:
                ov = _dot(jnp.exp2(sc - m).astype(bf16), vc_ref[:, vcols])
            o_ref[:, hcols] = (ov[:, :LANE] / ov[:, LANE:]).astype(bf16)


def _attention(q, k, v, latent):
    tq = 256
    vw = ATTN_KV_HEADS * V_EXT
    if latent:
        grid = (BATCH, SEQ // tq)
        qspec = pl.BlockSpec((None, tq, ATTN_Q_WIDTH), lambda b, t: (b, t, 0))
        ctx_kv = lambda width: pl.BlockSpec((None, CTX_LEN, width), lambda b, t: (BATCH, b, 0))
        lat_kv = lambda width: pl.BlockSpec((None, SEQ, width), lambda b, t: (b, 0, 0))
        in_specs = [qspec, ctx_kv(ATTN_KV_WIDTH), ctx_kv(vw), lat_kv(ATTN_KV_WIDTH), lat_kv(vw)]
        args = (q, k, v, k, v)
        out_shape = jax.ShapeDtypeStruct((BATCH, SEQ, ATTN_Q_WIDTH), bf16)
    else:
        grid = (BATCH, 1)
        qspec = pl.BlockSpec((None, tq, ATTN_Q_WIDTH), lambda b, t: (BATCH, b, 0))
        ctx_kv = lambda width: pl.BlockSpec((None, CTX_LEN, width), lambda b, t: (BATCH, b, 0))
        in_specs = [qspec, ctx_kv(ATTN_KV_WIDTH), ctx_kv(vw)]
        args = (q, k, v)
        out_shape = jax.ShapeDtypeStruct((BATCH, CTX_LEN, ATTN_Q_WIDTH), bf16)
    return pl.pallas_call(
        functools.partial(_attn_kernel, latent),
        grid=grid,
        in_specs=in_specs,
        out_specs=pl.BlockSpec((None, tq, ATTN_Q_WIDTH), lambda b, t: (b, t, 0)),
        out_shape=out_shape,
        compiler_params=_cparams("parallel", "parallel"),
        name="attention",
    )(*args)


def _out_kernel(n_in, with_ctx, *refs):
    per = 3 if with_ctx else 2
    h_ref, mod_ref, g_ref, b_ref, o_ref = refs[per * n_in:]
    _, _, gate = _mod_rows(mod_ref, 1)
    is_ctx = pl.program_id(0) == BATCH
    y = None
    for i in range(n_in):
        x = refs[per * i][...]
        if with_ctx:
            x = jnp.where(is_ctx, refs[per * i + 1][...], x)
        part = _dot(x.astype(bf16), refs[per * i + per - 1][...])
        y = part if y is None else y + part
    o_ref[...] = _layer_norm(DEEPNORM_ALPHA * h_ref[...] + gate * y, g_ref[...], b_ref[...])


def _out_proj(xs, w, h, mods, layer, ln_g, ln_b, with_ctx):
    n_slabs = NB if with_ctx else BATCH
    in_specs, args, row = [], [], 0
    for lat, ctx in xs:
        width = lat.shape[-1]
        in_specs.append(pl.BlockSpec((None, TOKEN_TILE, width), lambda b, t: (jnp.minimum(b, BATCH - 1), t, 0)))
        args.append(lat)
        if with_ctx:
            in_specs.append(pl.BlockSpec((None, TOKEN_TILE, width), lambda b, t: (0, t, 0)))
            args.append(ctx.reshape(1, SEQ, width))
        in_specs.append(pl.BlockSpec((width, D_MODEL), lambda b, t, r=row // width: (r, 0), pipeline_mode=pl.Buffered(1)))
        args.append(w)
        row += width
    return pl.pallas_call(
        functools.partial(_out_kernel, len(xs), with_ctx),
        grid=(n_slabs, SEQ // TOKEN_TILE),
        in_specs=in_specs + [_tok_spec(D_MODEL), _mod_spec(layer), *_ln_specs(layer, 1)],
        out_specs=_tok_spec(D_MODEL),
        out_shape=jax.ShapeDtypeStruct((n_slabs, SEQ, D_MODEL), f32),
        compiler_params=_cparams("parallel", "parallel"),
        name="out_proj",
    )(*args, h, mods, ln_g, ln_b)


def _dft_tables(length):
    idx = np.arange(length, dtype=np.int64)
    ang = (np.outer(idx, idx) % (2 * length)).astype(np.float64) * (math.pi / length)
    return tuple(jnp.asarray(t, dtype=f32).astype(bf16) for t in (np.cos(ang), -np.sin(ang)))


def _filter_kernel(length, feats_ref, t_ref, dl_ref, w1_ref, b1_ref, fr1_ref, w2_ref, b2_ref, fr2_ref,
                   w3_ref, b3_ref, cos_ref, sin_ref, kre_ref, kim_ref):
    h = jnp.sin(fr1_ref[...] * (_dot_f32(feats_ref[...], w1_ref[...]) + b1_ref[...]))
    h = jnp.sin(fr2_ref[...] * (_dot_f32(h, w2_ref[...]) + b2_ref[...]))
    h = _dot_f32(h, w3_ref[...]) + b3_ref[...]
    win = jnp.exp(-t_ref[...] * dl_ref[...])
    row = lax.broadcasted_iota(jnp.int32, (length, HYENA_WIDTH), 0)
    hf = h[:, :HYENA_WIDTH] * win
    hb = jnp.where(row == 0, 0.0, h[:, HYENA_WIDTH:] * win)
    inv = lax.rsqrt(jnp.sum(hf * hf + hb * hb, 0, keepdims=True) + 1e-6)
    even = (hf + hb) * inv
    odd = (hf - hb) * inv
    sign = jnp.where((row & 1) == 0, 1.0, -1.0)
    nyq = jnp.sum(even * sign, 0, keepdims=True)
    wgt = jnp.where(row == 0, 1.0, 2.0) * (0.5 / length)
    e0, e1, e2 = _split3(even)
    o0, o1, o2 = _split3(odd)
    cm, sm = cos_ref[...], sin_ref[...]
    kre = _dot(cm, e0) + _dot(cm, e1) + _dot(cm, e2)
    kim = _dot(sm, o0) + _dot(sm, o1) + _dot(sm, o2)
    kre_ref[...] = kre * wgt
    kim_ref[...] = jnp.where(row == 0, nyq, kim) * wgt


def _hyena_filter(length, w1, b1, fr1, w2, b2, fr2, w3, b3, cos_m, sin_m):
    n = np.arange(length, dtype=np.float32)[:, None]
    t = np.linspace(0.0, 1.0, length, dtype=np.float32)[:, None]
    bands = np.linspace(1e-4, HYENA_BANDS - 1, HYENA_BANDS, dtype=np.float32)[None, :]
    ang = (np.float32(2.0 * math.pi) * n * bands / np.float32(length)).astype(np.float32)
    feats = np.zeros((length, EMB_PAD), np.float32)
    feats[:, :HYENA_EMB] = np.concatenate([t, np.cos(ang), -np.sin(ang)], axis=-1)
    max_decay = math.log(HYENA_TARGET) / HYENA_FAST_DECAY
    min_decay = math.log(HYENA_TARGET) / HYENA_SLOW_DECAY
    deltas = np.abs(np.linspace(min_decay, max_decay, HYENA_WIDTH, dtype=np.float32))[None, :]
    w1p = jnp.zeros((EMB_PAD, HYENA_FILTER_HIDDEN), f32).at[:HYENA_EMB].set(w1)
    row = lambda v: v.reshape(1, -1)
    args = (jnp.asarray(feats), jnp.asarray(t), jnp.asarray(deltas), w1p, row(b1), row(fr1), w2, row(b2), row(fr2),
            w3, row(b3), cos_m, sin_m)
    out = jax.ShapeDtypeStruct((length, HYENA_WIDTH), f32)
    return pl.pallas_call(
        functools.partial(_filter_kernel, length),
        out_shape=[out, out],
        compiler_params=pltpu.CompilerParams(vmem_limit_bytes=VMEM_LIMIT),
        name="hyena_filter",
    )(*args)


def _conv3(p_ref, w_ref, b_ref, length):
    x = p_ref[...]
    row = lax.broadcasted_iota(jnp.int32, x.shape, 0)
    prev = jnp.where(row == 0, 0.0, pltpu.roll(x, 1, 0))
    nxt = jnp.where(row == length - 1, 0.0, pltpu.roll(x, length - 1, 0))
    return prev * w_ref[0:1, :] + x * w_ref[1:2, :] + nxt * w_ref[2:3, :] + b_ref[...]


def _hyena_kernel(length, p0_ref, p1_ref, p2_ref, w0_ref, w1_ref, w2_ref, b0_ref, b1_ref, b2_ref, skip_ref,
                  kre_ref, kim_ref, cos_ref, sin_ref, o_ref, yre_ref, yim_ref):
    z = _conv3(p0_ref, w0_ref, b0_ref, length) * _conv3(p1_ref, w1_ref, b1_ref, length)
    zb = z.astype(bf16)
    ft = min(length, HY_FT)
    for f0 in range(0, length, ft):
        rows = slice(f0, f0 + ft)
        zre = _dot(cos_ref[rows, :], zb)
        zim = _dot(sin_ref[rows, :], zb)
        kre, kim = kre_ref[rows, :], kim_ref[rows, :]
        yre_ref[rows, :] = (zre * kre - zim * kim).astype(bf16)
        yim_ref[rows, :] = (zre * kim + zim * kre).astype(bf16)
    row = lax.broadcasted_iota(jnp.int32, z.shape, 0)
    sign = jnp.where((row & 1) == 0, 1.0, -1.0)
    y_nyq = jnp.sum(z * sign, 0, keepdims=True) * kim_ref[0:1, :]
    y = _dot(cos_ref[...], yre_ref[...]) + _dot(sin_ref[...], yim_ref[...]) + sign * y_nyq
    o_ref[...] = (y + z * skip_ref[...]) * _conv3(p2_ref, w2_ref, b2_ref, length)


def _hyena(p, length, conv_w, conv_b, skip, kre, kim, cos_m, sin_m):
    nblk = HYENA_WIDTH // HY_CBLK
    if length == SEQ:
        part = lambda j: pl.BlockSpec((None, length, HY_CBLK), lambda c, b: (b, 0, j * nblk + c))
    else:
        part = lambda j: pl.BlockSpec((None, length, HY_CBLK), lambda c, b: (BATCH, b, j * nblk + c))
    wpart = lambda j, rows: pl.BlockSpec((rows, HY_CBLK), lambda c, b: (0, j * nblk + c))
    chan = pl.BlockSpec((1, HY_CBLK), lambda c, b: (0, c))
    spec_k = pl.BlockSpec((length, HY_CBLK), lambda c, b: (0, c), pipeline_mode=pl.Buffered(1))
    table = _resident((length, length))
    return pl.pallas_call(
        functools.partial(_hyena_kernel, length),
        grid=(nblk, BATCH),
        in_specs=[part(0), part(1), part(2), wpart(0, 3), wpart(1, 3), wpart(2, 3), wpart(0, 1), wpart(1, 1),
                  wpart(2, 1), chan, spec_k, spec_k, table, table],
        out_specs=pl.BlockSpec((None, length, HY_CBLK), lambda c, b: (b, 0, c)),
        out_shape=jax.ShapeDtypeStruct((BATCH, length, HYENA_WIDTH), f32),
        scratch_shapes=[pltpu.VMEM((length, HY_CBLK), bf16), pltpu.VMEM((length, HY_CBLK), bf16)],
        compiler_params=_cparams("parallel", "parallel"),
        name="hyena",
    )(p, p, p, conv_w, conv_w, conv_w, conv_b, conv_b, conv_b, skip, kre, kim, cos_m, sin_m)


GLA_GROUP_LAT = 4
GLA_GROUP_CTX = 2


def _log_sigmoid(x):
    return jnp.minimum(x, 0.0) - jnp.log(1.0 + jnp.exp(-jnp.abs(x)))


def _gla_block_prep(direction, refs, gup_ref, gb_ref, rb, group):
    q_ref, k_ref, v_ref, _, lo_ref = refs
    n_rows = group * GLA_CHUNK
    rows = pl.ds(rb, n_rows)
    ri = lax.broadcasted_iota(jnp.int32, (n_rows, n_rows), 0)
    ci = lax.broadcasted_iota(jnp.int32, (n_rows, n_rows), 1)
    same_chunk = (ri // GLA_CHUNK) == (ci // GLA_CHUNK)
    tri = same_chunk & ((ci <= ri) if direction == 0 else (ci >= ri))
    tri_b = jnp.where(tri, 1.0, 0.0).astype(bf16)
    x = _dot(lo_ref[rows, :].astype(bf16), gup_ref[direction]) + gb_ref[direction]
    lg = _log_sigmoid(x) * (1.0 / GLA_TAU)
    l0, l1, l2 = _split3(lg)
    cum = _dot(tri_b, l0) + _dot(tri_b, l1) + _dot(tri_b, l2)
    tots = []
    for g in range(group):
        end = g * GLA_CHUNK + (GLA_CHUNK - 1 if direction == 0 else 0)
        tots.append(cum[end:end + 1, :])
    tot_rows = jnp.concatenate([jnp.broadcast_to(t, (GLA_CHUNK, t.shape[1])) for t in tots], 0)
    k = k_ref[rows, :]
    q_in = (q_ref[rows, :] * (GLA_DK ** -0.5) * jnp.exp(cum)).astype(bf16)
    k_in = (k * jnp.exp(-cum)).astype(bf16)
    k_st = (k * jnp.exp(tot_rows - cum)).astype(bf16)
    v = v_ref[rows, :].astype(bf16)
    intra, ds_t = [], [[] for _ in range(group)]
    for hd in range(GLA_HEADS):
        qc = slice(hd * LANE, (hd + 1) * LANE)
        vc = slice(hd * GLA_DV, (hd + 1) * GLA_DV)
        att = jnp.where(tri, _dot_nt(q_in[:, qc], k_in[:, qc]), 0.0)
        intra.append(_dot(att.astype(bf16), v[:, vc]))
        for g in range(group):
            cr = slice(g * GLA_CHUNK, (g + 1) * GLA_CHUNK)
            ds_t[g].append(_dot_tn(v[cr, vc], k_st[cr, qc]))
    decs = [jnp.exp(t) for t in tots]
    return q_in, intra, ds_t, decs


def _gla_block_scan(direction, prep, st_ref, group):
    q_in, intra, ds_t, decs = prep
    order = range(group) if direction == 0 else range(group - 1, -1, -1)
    outs = []
    for hd in range(GLA_HEADS):
        qc = slice(hd * LANE, (hd + 1) * LANE)
        st = st_ref[direction, hd]
        inter = [None] * group
        for g in order:
            cr = slice(g * GLA_CHUNK, (g + 1) * GLA_CHUNK)
            inter[g] = _dot_nt(q_in[cr, qc], st.astype(bf16))
            st = st * decs[g][:, qc] + ds_t[g][hd]
        st_ref[direction, hd] = st
        outs.append(intra[hd] + jnp.concatenate(inter, 0))
    return jnp.concatenate(outs, axis=-1)


def _gla_finish(o, g, ng):
    outs = []
    for hd in range(GLA_HEADS):
        oh = o[:, hd * GLA_DV:(hd + 1) * GLA_DV]
        outs.append(oh * lax.rsqrt(jnp.mean(oh * oh, -1, keepdims=True) + RMS_EPS) * ng)
    return jnp.concatenate(outs, axis=-1) * (g * jax.nn.sigmoid(g))


def _gla_scan(refs, o_ref, gup_ref, gb_ref, ng_ref, st_ref, n_chunks, group):
    g_ref = refs[3]
    n_rows = group * GLA_CHUNK
    n_blocks = n_chunks // group
    half = n_blocks // 2

    def step(i, finish):
        bases = [pl.multiple_of(blk * n_rows, n_rows) for blk in (i, n_blocks - 1 - i)]
        preps = [_gla_block_prep(d, refs, gup_ref, gb_ref, bases[d], group) for d in (0, 1)]
        for d in (0, 1):
            rows = pl.ds(bases[d], n_rows)
            o = _gla_block_scan(d, preps[d], st_ref, group)
            if finish:
                o = _gla_finish(o_ref[rows, :] + o, g_ref[rows, :], ng_ref[...])
            o_ref[rows, :] = o

    def first(i, carry):
        step(i, False)
        return carry

    def second(i, carry):
        step(i, True)
        return carry

    lax.fori_loop(0, half, first, 0)
    lax.fori_loop(half, n_blocks, second, 0)


def _gla_kernel(*refs):
    lat, ctx = refs[0:5], refs[5:10]
    gup_ref, gb_ref, ng_ref, o_ref, oc_ref, st_ref = refs[10:]
    st_ref[...] = jnp.zeros_like(st_ref)
    _gla_scan(ctx, oc_ref, gup_ref, gb_ref, ng_ref, st_ref, CTX_LEN // GLA_CHUNK, GLA_GROUP_CTX)
    _gla_scan(lat, o_ref, gup_ref, gb_ref, ng_ref, st_ref, SEQ // GLA_CHUNK, GLA_GROUP_LAT)


def _gla(p, gup, gb, ng):
    def pieces(latent):
        ln = SEQ if latent else CTX_LEN
        if latent:
            blk = lambda width, off: pl.BlockSpec((None, ln, width), lambda b: (b, 0, off // width))
        else:
            blk = lambda width, off: pl.BlockSpec((None, ln, width), lambda b: (BATCH, b, off // width))
        return [blk(GLA_HEADS * LANE, EVEN_Q), blk(GLA_HEADS * LANE, EVEN_K), blk(GLA_WIDTH, EVEN_V),
                blk(GLA_WIDTH, EVEN_G), blk(LANE, EVEN_LO)]

    out = lambda ln: pl.BlockSpec((None, ln, GLA_WIDTH), lambda b: (b, 0, 0))
    return pl.pallas_call(
        _gla_kernel,
        grid=(BATCH,),
        in_specs=pieces(True) + pieces(False) + [_resident(gup.shape), _resident(gb.shape), _resident(ng.shape)],
        out_specs=[out(SEQ), out(CTX_LEN)],
        out_shape=[jax.ShapeDtypeStruct((BATCH, SEQ, GLA_WIDTH), f32),
                   jax.ShapeDtypeStruct((BATCH, CTX_LEN, GLA_WIDTH), f32)],
        scratch_shapes=[pltpu.VMEM((2, GLA_HEADS, GLA_DV, LANE), f32)],
        compiler_params=_cparams("parallel"),
        name="gla",
    )(*([p] * 10), gup, gb, ng)


def _even_w_in_padded(w_in):
    hy, q, k, v, g, af, ab = jnp.split(
        w_in, np.cumsum([HYENA_PROJ, GLA_QK, GLA_QK, GLA_WIDTH, GLA_WIDTH, GLA_RANK]).tolist(), axis=-1)

    def pad_heads(t):
        t = t.reshape(D_MODEL, GLA_HEADS, GLA_DK)
        return jnp.pad(t, ((0, 0), (0, 0), (0, LANE - GLA_DK))).reshape(D_MODEL, GLA_HEADS * LANE)

    lo = jnp.pad(jnp.concatenate([af, ab], -1), ((0, 0), (0, LANE - 2 * GLA_RANK)))
    return jnp.concatenate([hy, pad_heads(q), pad_heads(k), v, g, lo], -1).astype(bf16)


def _gla_gate_params(gate_up, gate_b):
    def pad_heads(t):
        t = t.reshape(t.shape[0], GLA_HEADS, GLA_DK)
        return jnp.pad(t, ((0, 0), (0, 0), (0, LANE - GLA_DK))).reshape(t.shape[0], GLA_HEADS * LANE)

    ups = []
    for j in range(2):
        up = jnp.zeros((LANE, GLA_HEADS * LANE), f32).at[j * GLA_RANK:(j + 1) * GLA_RANK].set(pad_heads(gate_up[j]))
        ups.append(up)
    return jnp.stack(ups).astype(bf16), pad_heads(gate_b).reshape(2, 1, GLA_HEADS * LANE)


_HEAD_PERM = np.concatenate([np.arange(0, 32), np.arange(64, 96), np.arange(32, 64), np.arange(96, 128)])


def _attn_w_in_permuted(w_in):
    nqk = ATTN_Q_HEADS + ATTN_KV_HEADS
    cols = (np.arange(nqk)[:, None] * ATTN_HEAD_DIM + _HEAD_PERM[None, :]).reshape(-1)
    cols = np.concatenate([cols, np.arange(nqk * ATTN_HEAD_DIM, nqk * ATTN_HEAD_DIM + ATTN_KV_WIDTH)])
    return w_in[:, cols].astype(bf16)


def _rope_tables():
    rows = SEQ // GRID_W
    row = jnp.repeat(jnp.arange(rows, dtype=f32), GRID_W)
    col = jnp.tile(jnp.arange(GRID_W, dtype=f32), rows)
    inv_freq = ROPE_THETA ** (-jnp.arange(ROPE_AXIS_PAIRS, dtype=f32) / ROPE_AXIS_PAIRS)
    ang_r = row[:, None] * inv_freq
    ang_c = col[:, None] * inv_freq
    cos = jnp.concatenate([jnp.cos(ang_r), jnp.cos(ang_c), jnp.cos(ang_r), jnp.cos(ang_c)], -1)
    sin = jnp.concatenate([-jnp.sin(ang_r), -jnp.sin(ang_c), jnp.sin(ang_r), jnp.sin(ang_c)], -1)
    return cos, sin


def kernel(x, c, ctx, c_ctx, ada_w, ada_b, ln_g, ln_b, ffn_w1, ffn_w2, even_w_in, even_w_out, hyena_conv_w, hyena_conv_b, hyena_f_w1, hyena_f_b1, hyena_f_fr1, hyena_f_w2, hyena_f_b2, hyena_f_fr2, hyena_f_w3, hyena_f_b3, hyena_skip, gla_gate_up, gla_gate_b, gla_norm_g, attn_w_in, attn_w_out, attn_q_norm, attn_k_norm):
    c_all = jnp.concatenate([c, c_ctx[None], jnp.zeros((MOD_ROWS - NB, D_MODEL), f32)], 0)
    mods = _adaln(c_all, ada_w, ada_b).reshape(DEPTH, MOD_ROWS, N_MOD, D_MODEL)
    cos, sin = _rope_tables()
    dft_lat = _dft_tables(SEQ)
    dft_ctx = _dft_tables(CTX_LEN)
    w1 = ffn_w1.astype(bf16)
    w2 = ffn_w2.astype(bf16)
    lng = ln_g.reshape(DEPTH, 3, 1, D_MODEL)
    lnb = ln_b.reshape(DEPTH, 3, 1, D_MODEL)

    h = jnp.concatenate([x, ctx.reshape(1, SEQ, D_MODEL)], 0)
    for i in range(DEPTH):
        need_ctx = i < DEPTH - 1
        h = _ffn(h, mods, i, 0, 0, w1, w2, lng, lnb, NB)
        if i % 2 == 0:
            e = i // 2
            w_out = even_w_out[e].astype(bf16)
            gup, gb = _gla_gate_params(gla_gate_up[e], gla_gate_b[e])
            p = _proj(h, mods, i, _even_w_in_padded(even_w_in[e]))
            filt = (hyena_f_w1[e], hyena_f_b1[e], hyena_f_fr1[e], hyena_f_w2[e], hyena_f_b2[e], hyena_f_fr2[e],
                    hyena_f_w3[e], hyena_f_b3[e])
            conv_b = hyena_conv_b[e].reshape(1, HYENA_PROJ)
            skip = hyena_skip[e].reshape(1, HYENA_WIDTH)
            kre, kim = _hyena_filter(SEQ, *filt, dft_lat[0], dft_lat[1])
            yh = _hyena(p, SEQ, hyena_conv_w[e], conv_b, skip, kre, kim, *dft_lat)
            kre_c, kim_c = _hyena_filter(CTX_LEN, *filt, dft_ctx[0], dft_ctx[1])
            yhc = _hyena(p, CTX_LEN, hyena_conv_w[e], conv_b, skip, kre_c, kim_c, *dft_ctx)
            yg, ygc = _gla(p, gup, gb, gla_norm_g[e].reshape(1, GLA_DV))
            xs = [(yh, yhc), (yg, ygc)]
        else:
            o = i // 2
            w_out = attn_w_out[o].astype(bf16)
            qn = attn_q_norm[o][_HEAD_PERM].reshape(1, LANE)
            kn = attn_k_norm[o][_HEAD_PERM].reshape(1, LANE)
            q, k, v = _attn_proj(h, mods, i, _attn_w_in_permuted(attn_w_in[o]), qn, kn, cos, sin)
            att = _attention(q, k, v, True)
            attc = _attention(q, k, v, False) if need_ctx else None
            xs = [(att, attc)]
        n_slabs = NB if need_ctx else BATCH
        h = _out_proj(xs, w_out, h, mods, i, lng, lnb, need_ctx)
        h = _ffn(h, mods, i, 2, 1, w1, w2, lng, lnb, n_slabs)
    return h
```

```python
import functools
import math

import jax
import jax.numpy as jnp
import numpy as np
from jax import lax
from jax.experimental import pallas as pl
from jax.experimental.pallas import tpu as pltpu

f32 = jnp.float32
bf16 = jnp.bfloat16

D_MODEL = 1024
BATCH = 8
SEQ = 2048
DEPTH = 4
GRID_W = 64
CTX_LEN = 256
N_MOD = 9
DEEPNORM_ALPHA = (2.0 * DEPTH) ** 0.25
LN_EPS = 1e-5
RMS_EPS = 1e-6
MACARON_WEIGHT = 0.5
FFN_HIDDEN = 2816

HYENA_WIDTH = 512
HYENA_PROJ = 3 * HYENA_WIDTH
HYENA_BANDS = 16
HYENA_EMB = 1 + 2 * HYENA_BANDS
HYENA_FILTER_HIDDEN = 64
HYENA_FAST_DECAY = 0.3
HYENA_SLOW_DECAY = 1.5
HYENA_TARGET = 1e-2

GLA_WIDTH = 512
GLA_HEADS = 4
GLA_DV = 128
GLA_DK = 64
GLA_QK = GLA_HEADS * GLA_DK
GLA_RANK = 16
GLA_TAU = 16.0
GLA_CHUNK = 64

ATTN_HEAD_DIM = 128
ATTN_Q_HEADS = 8
ATTN_KV_HEADS = 2
ATTN_GROUP = 4
ATTN_Q_WIDTH = 1024
ATTN_KV_WIDTH = 256
ROPE_THETA = 10000.0
ROPE_AXIS_PAIRS = 32

assert BATCH * CTX_LEN == SEQ
NB = BATCH + 1

LANE = 128
SUBLANE = 8
MOD_ROWS = 16
EMB_PAD = 128
VMEM_LIMIT = 56 * 1024 * 1024
TOKEN_TILE = 512

EVEN_Q = HYENA_PROJ
EVEN_K = EVEN_Q + GLA_QK
EVEN_V = EVEN_K + GLA_QK
EVEN_G = EVEN_V + GLA_WIDTH
EVEN_LO = EVEN_G + GLA_WIDTH
EVEN_W = EVEN_LO + LANE
HY_CBLK = 256
HY_FT = 512
V_EXT = 2 * LANE


def _cparams(*sem):
    return pltpu.CompilerParams(dimension_semantics=sem, vmem_limit_bytes=VMEM_LIMIT)


def _resident(shape, index=None):
    index = tuple(index or ())
    block = (None,) * len(index) + tuple(shape[len(index):])
    tail = (0,) * (len(shape) - len(index))
    return pl.BlockSpec(block, lambda *_: index + tail, pipeline_mode=pl.Buffered(1))


def _dot(a, b):
    return jnp.dot(a, b, preferred_element_type=f32)


def _dot_nt(a, b):
    return lax.dot_general(a, b, (((1,), (1,)), ((), ())), preferred_element_type=f32)


def _dot_tn(a, b):
    return lax.dot_general(a, b, (((0,), (0,)), ((), ())), preferred_element_type=f32)


def _split3(x):
    hi = x.astype(bf16)
    r1 = x - hi.astype(f32)
    mid = r1.astype(bf16)
    lo = (r1 - mid.astype(f32)).astype(bf16)
    return hi, mid, lo


def _dot_f32(a, b):
    a0, a1, a2 = _split3(a)
    b0, b1, b2 = _split3(b)
    return (_dot(a0, b0) + (_dot(a0, b1) + _dot(a1, b0))
            + (_dot(a0, b2) + _dot(a1, b1) + _dot(a2, b0)))


def _layer_norm(x, g, b):
    mu = jnp.mean(x, -1, keepdims=True)
    xc = x - mu
    var = jnp.mean(xc * xc, -1, keepdims=True)
    return xc * lax.rsqrt(var + LN_EPS) * g + b


def _mod_rows(mod_ref, slot):
    return (mod_ref[3 * slot:3 * slot + 1, :], mod_ref[3 * slot + 1:3 * slot + 2, :],
            mod_ref[3 * slot + 2:3 * slot + 3, :])


def _mod_spec(layer):
    return pl.BlockSpec((None, None, N_MOD, D_MODEL), lambda b, *_: (layer, b, 0, 0))


def _tok_spec(width):
    return pl.BlockSpec((None, TOKEN_TILE, width), lambda b, t: (b, t, 0))


def _ln_specs(layer, slot):
    return [_resident((DEPTH, 3, 1, D_MODEL), (layer, slot))] * 2


def _adaln_kernel(c_ref, w_ref, b_ref, o_ref):
    c = c_ref[...]
    sc = (c * jax.nn.sigmoid(c)).astype(bf16)
    o_ref[...] = _dot(sc, w_ref[...].astype(bf16)) + b_ref[...]


def _adaln(c_all, ada_w, ada_b):
    return pl.pallas_call(
        _adaln_kernel,
        grid=(DEPTH, N_MOD),
        in_specs=[
            pl.BlockSpec((MOD_ROWS, D_MODEL), lambda l, j: (0, 0)),
            pl.BlockSpec((None, D_MODEL, D_MODEL), lambda l, j: (l, 0, j)),
            pl.BlockSpec((None, 1, D_MODEL), lambda l, j: (l, 0, j)),
        ],
        out_specs=pl.BlockSpec((None, MOD_ROWS, D_MODEL), lambda l, j: (l, 0, j)),
        out_shape=jax.ShapeDtypeStruct((DEPTH, MOD_ROWS, N_MOD * D_MODEL), f32),
        compiler_params=_cparams("parallel", "parallel"),
        name="adaln",
    )(c_all, ada_w, ada_b.reshape(DEPTH, 1, N_MOD * D_MODEL))


FFN_TH = 256


def _ffn_sublayer(h, mod_ref, slot, w1_ref, w2_ref, g_ref, b_ref):
    shift, scale, gate = _mod_rows(mod_ref, slot)
    a = (h * (1.0 + scale) + shift).astype(bf16)
    acc = None
    for lo in range(0, FFN_HIDDEN, FFN_TH):
        gt = _dot(a, w1_ref[:, lo:lo + FFN_TH].astype(bf16))
        up = _dot(a, w1_ref[:, FFN_HIDDEN + lo:FFN_HIDDEN + lo + FFN_TH].astype(bf16))
        act = (gt * jax.nn.sigmoid(gt) * up).astype(bf16)
        part = _dot(act, w2_ref[lo:lo + FFN_TH, :].astype(bf16))
        acc = part if acc is None else acc + part
    return _layer_norm(DEEPNORM_ALPHA * h + gate * (MACARON_WEIGHT * acc), g_ref[...], b_ref[...])


def _ffn_kernel(slot, h_ref, mod_ref, w1_ref, w2_ref, g_ref, b_ref, o_ref):
    o_ref[...] = _ffn_sublayer(h_ref[...], mod_ref, slot, w1_ref, w2_ref, g_ref, b_ref)


def _ffn(h, mods, layer, slot, which, w1, w2, ln_g, ln_b, n_slabs):
    return pl.pallas_call(
        functools.partial(_ffn_kernel, slot),
        grid=(n_slabs, SEQ // TOKEN_TILE),
        in_specs=[_tok_spec(D_MODEL), _mod_spec(layer), _resident(w1.shape, (layer, which)),
                  _resident(w2.shape, (layer, which)), *_ln_specs(layer, slot)],
        out_specs=_tok_spec(D_MODEL),
        out_shape=jax.ShapeDtypeStruct((n_slabs, SEQ, D_MODEL), f32),
        compiler_params=_cparams("parallel", "parallel"),
        name="ffn",
    )(h, mods, w1, w2, ln_g, ln_b)


def _mix_ffn_kernel(n_in, *refs):
    x_refs, wo_refs = refs[:n_in], refs[n_in:2 * n_in]
    h_ref, mod_ref, g1_ref, b1_ref, w1_ref, w2_ref, g2_ref, b2_ref, o_ref = refs[2 * n_in:]
    _, _, gate = _mod_rows(mod_ref, 1)
    y = None
    for x_ref, wo_ref in zip(x_refs, wo_refs):
        part = _dot(x_ref[...], wo_ref[...])
        y = part if y is None else y + part
    h1 = _layer_norm(DEEPNORM_ALPHA * h_ref[...] + gate * y, g1_ref[...], b1_ref[...])
    o_ref[...] = _ffn_sublayer(h1, mod_ref, 2, w1_ref, w2_ref, g2_ref, b2_ref)


def _mix_ffn(xs, w_out, h, mods, layer, w1, w2, ln_g, ln_b, n_slabs):
    in_specs, row = [], 0
    for x in xs:
        in_specs.append(_tok_spec(x.shape[-1]))
    for x in xs:
        width = x.shape[-1]
        in_specs.append(pl.BlockSpec((width, D_MODEL), lambda b, t, r=row // width: (r, 0),
                                     pipeline_mode=pl.Buffered(1)))
        row += width
    return pl.pallas_call(
        functools.partial(_mix_ffn_kernel, len(xs)),
        grid=(n_slabs, SEQ // TOKEN_TILE),
        in_specs=in_specs + [_tok_spec(D_MODEL), _mod_spec(layer), *_ln_specs(layer, 1),
                             _resident(w1.shape, (layer, 1)), _resident(w2.shape, (layer, 1)), *_ln_specs(layer, 2)],
        out_specs=_tok_spec(D_MODEL),
        out_shape=jax.ShapeDtypeStruct((n_slabs, SEQ, D_MODEL), f32),
        compiler_params=_cparams("parallel", "parallel"),
        name="mix_ffn",
    )(*xs, *([w_out] * len(xs)), h, mods, ln_g, ln_b, w1, w2, ln_g, ln_b)


def _proj_kernel(tn, h_ref, mod_ref, w_ref, o_ref):
    shift, scale, _ = _mod_rows(mod_ref, 1)
    a = (h_ref[...] * (1.0 + scale) + shift).astype(bf16)
    n = w_ref.shape[1]
    for j in range(0, n, tn):
        w = min(tn, n - j)
        o_ref[:, j:j + w] = _dot(a, w_ref[:, j:j + w])


def _proj(h, mods, layer, w):
    n = w.shape[1]
    return pl.pallas_call(
        functools.partial(_proj_kernel, 512),
        grid=(NB, SEQ // TOKEN_TILE),
        in_specs=[_tok_spec(D_MODEL), _mod_spec(layer), _resident(w.shape)],
        out_specs=_tok_spec(n),
        out_shape=jax.ShapeDtypeStruct((NB, SEQ, n), f32),
        compiler_params=_cparams("parallel", "parallel"),
        name="even_proj",
    )(h, mods, w)


ATTN_PROJ_TILE = 256
Q_PRESCALE = ATTN_HEAD_DIM ** -0.5 * math.log2(math.e)


def _attn_proj_kernel(h_ref, mod_ref, w_ref, qn_ref, kn_ref, cos_ref, sin_ref, q_ref, k_ref, v_ref):
    shift, scale, _ = _mod_rows(mod_ref, 1)
    a = (h_ref[...] * (1.0 + scale) + shift).astype(bf16)
    is_ctx = pl.program_id(0) == BATCH
    cos = jnp.where(is_ctx, 1.0, cos_ref[...])
    sin = jnp.where(is_ctx, 0.0, sin_ref[...])

    def norm_rope(x, gain):
        x = x * lax.rsqrt(jnp.mean(x * x, -1, keepdims=True) + RMS_EPS) * gain
        return x * cos + pltpu.roll(x, ATTN_HEAD_DIM // 2, 1) * sin

    for half in range(2):
        x4 = _dot(a, w_ref[:, half * 512:(half + 1) * 512])
        for hh in range(4):
            hd = half * 4 + hh
            q_ref[:, hd * LANE:(hd + 1) * LANE] = (
                norm_rope(x4[:, hh * LANE:(hh + 1) * LANE], qn_ref[...]) * Q_PRESCALE).astype(bf16)
    kv = _dot(a, w_ref[:, ATTN_Q_WIDTH:ATTN_Q_WIDTH + 2 * ATTN_KV_WIDTH])
    ones = jnp.ones((a.shape[0], LANE), bf16)
    for hd in range(ATTN_KV_HEADS):
        k_ref[:, hd * LANE:(hd + 1) * LANE] = norm_rope(kv[:, hd * LANE:(hd + 1) * LANE], kn_ref[...]).astype(bf16)
        v_ref[:, hd * V_EXT:hd * V_EXT + LANE] = kv[:, ATTN_KV_WIDTH + hd * LANE:ATTN_KV_WIDTH + (hd + 1) * LANE].astype(bf16)
        v_ref[:, hd * V_EXT + LANE:(hd + 1) * V_EXT] = ones


def _attn_proj(h, mods, layer, w, qn, kn, cos, sin):
    tm = ATTN_PROJ_TILE
    tok = lambda width: pl.BlockSpec((None, tm, width), lambda b, t: (b, t, 0))
    rope = pl.BlockSpec((tm, LANE), lambda b, t: (t, 0))
    return pl.pallas_call(
        _attn_proj_kernel,
        grid=(NB, SEQ // tm),
        in_specs=[tok(D_MODEL), _mod_spec(layer), _resident(w.shape), _resident((1, LANE)), _resident((1, LANE)),
                  rope, rope],
        out_specs=[tok(ATTN_Q_WIDTH), tok(ATTN_KV_WIDTH), tok(ATTN_KV_HEADS * V_EXT)],
        out_shape=[jax.ShapeDtypeStruct((NB, SEQ, ATTN_Q_WIDTH), bf16),
                   jax.ShapeDtypeStruct((NB, SEQ, ATTN_KV_WIDTH), bf16),
                   jax.ShapeDtypeStruct((NB, SEQ, ATTN_KV_HEADS * V_EXT), bf16)],
        compiler_params=_cparams("parallel", "parallel"),
        name="attn_proj",
    )(h, mods, w, qn, kn, cos, sin)


def _attn_kernel(has_lat, q_ref, kc_ref, vc_ref, *rest):
    if has_lat:
        k_ref, v_ref, o_ref = rest
    else:
        (o_ref,) = rest
    for g in range(ATTN_KV_HEADS):
        cols = slice(g * LANE, (g + 1) * LANE)
        vcols = slice(g * V_EXT, (g + 1) * V_EXT)
        for hh in range(ATTN_GROUP):
            hcols = slice((g * ATTN_GROUP + hh) * LANE, (g * ATTN_GROUP + hh + 1) * LANE)
            q = q_ref[:, hcols]
            sc = _dot_nt(q, kc_ref[:, cols])
            m = jnp.max(sc, -1, keepdims=True)
            if has_lat:
                s = _dot_nt(q, k_ref[:, cols])
                m = jnp.maximum(m, jnp.max(s, -1, keepdims=True))
                ov = _dot(jnp.exp2(s - m).astype(bf16), v_ref[:, vcols])
                ov = ov + _dot(jnp.exp2(sc - m).astype(bf16), vc_ref[:, vcols])
            else:
                ov = _dot(jnp.exp2(sc - m).astype(bf16), vc_ref[:, vcols])
            o_ref[:, hcols] = (ov[:, :LANE] / ov[:, LANE:]).astype(bf16)


def _attention(q, k, v, latent):
    tq = 256
    vw = ATTN_KV_HEADS * V_EXT
    if latent:
        grid = (BATCH, SEQ // tq)
        qspec = pl.BlockSpec((None, tq, ATTN_Q_WIDTH), lambda b, t: (b, t, 0))
        ctx_kv = lambda width: pl.BlockSpec((None, CTX_LEN, width), lambda b, t: (BATCH, b, 0))
        lat_kv = lambda width: pl.BlockSpec((None, SEQ, width), lambda b, t: (b, 0, 0))
        in_specs = [qspec, ctx_kv(ATTN_KV_WIDTH), ctx_kv(vw), lat_kv(ATTN_KV_WIDTH), lat_kv(vw)]
        args = (q, k, v, k, v)
        out_shape = jax.ShapeDtypeStruct((NB, SEQ, ATTN_Q_WIDTH), bf16)
    else:
        grid = (BATCH, 1)
        qspec = pl.BlockSpec((None, tq, ATTN_Q_WIDTH), lambda b, t: (BATCH, b, 0))
        ctx_kv = lambda width: pl.BlockSpec((None, CTX_LEN, width), lambda b, t: (BATCH, b, 0))
        in_specs = [qspec, ctx_kv(ATTN_KV_WIDTH), ctx_kv(vw)]
        args = (q, k, v)
        out_shape = jax.ShapeDtypeStruct((BATCH, CTX_LEN, ATTN_Q_WIDTH), bf16)
    return pl.pallas_call(
        functools.partial(_attn_kernel, latent),
        grid=grid,
        in_specs=in_specs,
        out_specs=pl.BlockSpec((None, tq, ATTN_Q_WIDTH), lambda b, t: (b, t, 0)),
        out_shape=out_shape,
        compiler_params=_cparams("parallel", "parallel"),
        name="attention",
    )(*args)


def _with_ctx_slab(lat, ctx):
    return lat.at[BATCH].set(ctx.reshape(SEQ, ctx.shape[-1]))


def _dft_tables(length):
    idx = np.arange(length, dtype=np.int64)
    ang = (np.outer(idx, idx) % (2 * length)).astype(np.float64) * (math.pi / length)
    return tuple(jnp.asarray(t, dtype=f32).astype(bf16) for t in (np.cos(ang), -np.sin(ang)))


def _filter_kernel(length, feats_ref, t_ref, dl_ref, w1_ref, b1_ref, fr1_ref, w2_ref, b2_ref, fr2_ref,
                   w3_ref, b3_ref, cos_ref, sin_ref, kre_ref, kim_ref):
    h = jnp.sin(fr1_ref[...] * (_dot_f32(feats_ref[...], w1_ref[...]) + b1_ref[...]))
    h = jnp.sin(fr2_ref[...] * (_dot_f32(h, w2_ref[...]) + b2_ref[...]))
    h = _dot_f32(h, w3_ref[...]) + b3_ref[...]
    win = jnp.exp(-t_ref[...] * dl_ref[...])
    row = lax.broadcasted_iota(jnp.int32, (length, HYENA_WIDTH), 0)
    hf = h[:, :HYENA_WIDTH] * win
    hb = jnp.where(row == 0, 0.0, h[:, HYENA_WIDTH:] * win)
    inv = lax.rsqrt(jnp.sum(hf * hf + hb * hb, 0, keepdims=True) + 1e-6)
    even = (hf + hb) * inv
    odd = (hf - hb) * inv
    sign = jnp.where((row & 1) == 0, 1.0, -1.0)
    nyq = jnp.sum(even * sign, 0, keepdims=True)
    wgt = jnp.where(row == 0, 1.0, 2.0) * (0.5 / length)
    e0, e1, e2 = _split3(even)
    o0, o1, o2 = _split3(odd)
    cm, sm = cos_ref[...], sin_ref[...]
    kre = _dot(cm, e0) + _dot(cm, e1) + _dot(cm, e2)
    kim = _dot(sm, o0) + _dot(sm, o1) + _dot(sm, o2)
    kre_ref[...] = kre * wgt
    kim_ref[...] = jnp.where(row == 0, nyq, kim) * wgt


def _hyena_filter(length, w1, b1, fr1, w2, b2, fr2, w3, b3, cos_m, sin_m):
    n = np.arange(length, dtype=np.float32)[:, None]
    t = np.linspace(0.0, 1.0, length, dtype=np.float32)[:, None]
    bands = np.linspace(1e-4, HYENA_BANDS - 1, HYENA_BANDS, dtype=np.float32)[None, :]
    ang = (np.float32(2.0 * math.pi) * n * bands / np.float32(length)).astype(np.float32)
    feats = np.zeros((length, EMB_PAD), np.float32)
    feats[:, :HYENA_EMB] = np.concatenate([t, np.cos(ang), -np.sin(ang)], axis=-1)
    max_decay = math.log(HYENA_TARGET) / HYENA_FAST_DECAY
    min_decay = math.log(HYENA_TARGET) / HYENA_SLOW_DECAY
    deltas = np.abs(np.linspace(min_decay, max_decay, HYENA_WIDTH, dtype=np.float32))[None, :]
    w1p = jnp.zeros((EMB_PAD, HYENA_FILTER_HIDDEN), f32).at[:HYENA_EMB].set(w1)
    row = lambda v: v.reshape(1, -1)
    args = (jnp.asarray(feats), jnp.asarray(t), jnp.asarray(deltas), w1p, row(b1), row(fr1), w2, row(b2), row(fr2),
            w3, row(b3), cos_m, sin_m)
    out = jax.ShapeDtypeStruct((length, HYENA_WIDTH), f32)
    return pl.pallas_call(
        functools.partial(_filter_kernel, length),
        out_shape=[out, out],
        compiler_params=pltpu.CompilerParams(vmem_limit_bytes=VMEM_LIMIT),
        name="hyena_filter",
    )(*args)


def _conv3(p_ref, w_ref, b_ref, length):
    x = p_ref[...]
    row = lax.broadcasted_iota(jnp.int32, x.shape, 0)
    prev = jnp.where(row == 0, 0.0, pltpu.roll(x, 1, 0))
    nxt = jnp.where(row == length - 1, 0.0, pltpu.roll(x, length - 1, 0))
    return prev * w_ref[0:1, :] + x * w_ref[1:2, :] + nxt * w_ref[2:3, :] + b_ref[...]


def _hyena_kernel(length, p0_ref, p1_ref, p2_ref, w0_ref, w1_ref, w2_ref, b0_ref, b1_ref, b2_ref, skip_ref,
                  kre_ref, kim_ref, cos_ref, sin_ref, o_ref, yre_ref, yim_ref):
    z = _conv3(p0_ref, w0_ref, b0_ref, length) * _conv3(p1_ref, w1_ref, b1_ref, length)
    zb = z.astype(bf16)
    ft = min(length, HY_FT)
    for f0 in range(0, length, ft):
        rows = slice(f0, f0 + ft)
        zre = _dot(cos_ref[rows, :], zb)
        zim = _dot(sin_ref[rows, :], zb)
        kre, kim = kre_ref[rows, :], kim_ref[rows, :]
        yre_ref[rows, :] = (zre * kre - zim * kim).astype(bf16)
        yim_ref[rows, :] = (zre * kim + zim * kre).astype(bf16)
    row = lax.broadcasted_iota(jnp.int32, z.shape, 0)
    sign = jnp.where((row & 1) == 0, 1.0, -1.0)
    y_nyq = jnp.sum(z * sign, 0, keepdims=True) * kim_ref[0:1, :]
    y = _dot(cos_ref[...], yre_ref[...]) + _dot(sin_ref[...], yim_ref[...]) + sign * y_nyq
    o_ref[...] = ((y + z * skip_ref[...]) * _conv3(p2_ref, w2_ref, b2_ref, length)).astype(bf16)


def _hyena(p, length, conv_w, conv_b, skip, kre, kim, cos_m, sin_m):
    nblk = HYENA_WIDTH // HY_CBLK
    n_out = NB if length == SEQ else BATCH
    if length == SEQ:
        part = lambda j: pl.BlockSpec((None, length, HY_CBLK), lambda c, b: (b, 0, j * nblk + c))
    else:
        part = lambda j: pl.BlockSpec((None, length, HY_CBLK), lambda c, b: (BATCH, b, j * nblk + c))
    wpart = lambda j, rows: pl.BlockSpec((rows, HY_CBLK), lambda c, b: (0, j * nblk + c))
    chan = pl.BlockSpec((1, HY_CBLK), lambda c, b: (0, c))
    spec_k = pl.BlockSpec((length, HY_CBLK), lambda c, b: (0, c), pipeline_mode=pl.Buffered(1))
    table = _resident((length, length))
    return pl.pallas_call(
        functools.partial(_hyena_kernel, length),
        grid=(nblk, BATCH),
        in_specs=[part(0), part(1), part(2), wpart(0, 3), wpart(1, 3), wpart(2, 3), wpart(0, 1), wpart(1, 1),
                  wpart(2, 1), chan, spec_k, spec_k, table, table],
        out_specs=pl.BlockSpec((None, length, HY_CBLK), lambda c, b: (b, 0, c)),
        out_shape=jax.ShapeDtypeStruct((n_out, length, HYENA_WIDTH), bf16),
        scratch_shapes=[pltpu.VMEM((length, HY_CBLK), bf16), pltpu.VMEM((length, HY_CBLK), bf16)],
        compiler_params=_cparams("parallel", "parallel"),
        name="hyena",
    )(p, p, p, conv_w, conv_w, conv_w, conv_b, conv_b, conv_b, skip, kre, kim, cos_m, sin_m)


GLA_GROUP_LAT = 4
GLA_GROUP_CTX = 2
GLA_HEADS_PER_TILE = LANE // GLA_DK
GLA_PRE_TILE = 256


def _log_sigmoid(x):
    return jnp.minimum(x, 0.0) - jnp.log(1.0 + jnp.exp(-jnp.abs(x)))


def _head_tile(hd):
    tile = hd // GLA_HEADS_PER_TILE
    lane = lax.broadcasted_iota(jnp.int32, (1, LANE), 1)
    first = (hd % GLA_HEADS_PER_TILE) * GLA_DK
    return slice(tile * LANE, (tile + 1) * LANE), (lane >= first) & (lane < first + GLA_DK)


def _aligned(x, m):
    return x if isinstance(x, int) else pl.multiple_of(x, m)


def _chunk_tri(n_rows, direction):
    ri = lax.broadcasted_iota(jnp.int32, (n_rows, n_rows), 0)
    ci = lax.broadcasted_iota(jnp.int32, (n_rows, n_rows), 1)
    same_chunk = (ri // GLA_CHUNK) == (ci // GLA_CHUNK)
    return same_chunk & ((ci <= ri) if direction == 0 else (ci >= ri))


def _gla_decay_tile(direction, refs, gup_ref, gb_ref, scr, rb, sb):
    q_ref, k_ref, _, _, lo_ref = refs
    qin_ref, kin_ref, kst_ref, dec_ref = scr
    group = GLA_PRE_TILE // GLA_CHUNK
    rows = pl.ds(rb, GLA_PRE_TILE)
    tri_b = jnp.where(_chunk_tri(GLA_PRE_TILE, direction), 1.0, 0.0).astype(bf16)
    x = _dot(lo_ref[rows, :].astype(bf16), gup_ref[direction]) + gb_ref[direction]
    lg = _log_sigmoid(x) * (1.0 / GLA_TAU)
    l0, l1, _ = _split3(lg)
    cum = _dot(tri_b, l0) + _dot(tri_b, l1)
    tots = []
    for g in range(group):
        end = g * GLA_CHUNK + (GLA_CHUNK - 1 if direction == 0 else 0)
        tots.append(cum[end:end + 1, :])
    tot_rows = jnp.concatenate([jnp.broadcast_to(t, (GLA_CHUNK, t.shape[1])) for t in tots], 0)
    k = k_ref[rows, :]
    srows = pl.ds(sb, GLA_PRE_TILE)
    qin_ref[direction, srows, :] = (q_ref[rows, :] * (GLA_DK ** -0.5) * jnp.exp(cum)).astype(bf16)
    kin_ref[direction, srows, :] = (k * jnp.exp(-cum)).astype(bf16)
    kst_ref[direction, srows, :] = (k * jnp.exp(tot_rows - cum)).astype(bf16)
    for g in range(group):
        drow = _aligned((sb // GLA_CHUNK + g) * SUBLANE, SUBLANE)
        dec_ref[direction, pl.ds(drow, SUBLANE), :] = jnp.broadcast_to(jnp.exp(tots[g]), (SUBLANE, GLA_QK))


def _gla_block_prep(direction, refs, scr, rb, sb, group):
    v_ref = refs[2]
    qin_ref, kin_ref, kst_ref, dec_ref = scr
    n_rows = group * GLA_CHUNK
    tri = _chunk_tri(n_rows, direction)
    srows = pl.ds(sb, n_rows)
    q_in, k_in, k_st = qin_ref[direction, srows, :], kin_ref[direction, srows, :], kst_ref[direction, srows, :]
    dec = dec_ref[direction, pl.ds(_aligned(sb // GLA_CHUNK * SUBLANE, group * SUBLANE), group * SUBLANE), :]
    decs = [dec[g * SUBLANE:g * SUBLANE + 1, :] for g in range(group)]
    v = v_ref[pl.ds(rb, n_rows), :].astype(bf16)
    q_heads, intra, ds_t = [], [], [[] for _ in range(group)]
    for hd in range(GLA_HEADS):
        cols, lanes = _head_tile(hd)
        vc = slice(hd * GLA_DV, (hd + 1) * GLA_DV)
        q_h = jnp.where(lanes, q_in[:, cols], 0.0).astype(bf16)
        kst_h = jnp.where(lanes, k_st[:, cols], 0.0).astype(bf16)
        att = jnp.where(tri, _dot_nt(q_h, k_in[:, cols]), 0.0)
        q_heads.append(q_h)
        intra.append(_dot(att.astype(bf16), v[:, vc]))
        for g in range(group):
            cr = slice(g * GLA_CHUNK, (g + 1) * GLA_CHUNK)
            ds_t[g].append(_dot_tn(v[cr, vc], kst_h[cr, :]))
    return q_heads, intra, ds_t, decs


def _gla_block_scan(direction, prep, st_ref, group):
    q_heads, intra, ds_t, decs = prep
    order = range(group) if direction == 0 else range(group - 1, -1, -1)
    outs = []
    for hd in range(GLA_HEADS):
        cols, _ = _head_tile(hd)
        st = st_ref[direction, hd]
        inter = [None] * group
        for g in order:
            cr = slice(g * GLA_CHUNK, (g + 1) * GLA_CHUNK)
            inter[g] = _dot_nt(q_heads[hd][cr, :], st.astype(bf16))
            st = st * decs[g][:, cols] + ds_t[g][hd]
        st_ref[direction, hd] = st
        outs.append(intra[hd] + jnp.concatenate(inter, 0))
    return jnp.concatenate(outs, axis=-1)


def _gla_finish(o, g, ng):
    outs = []
    for hd in range(GLA_HEADS):
        oh = o[:, hd * GLA_DV:(hd + 1) * GLA_DV]
        outs.append(oh * lax.rsqrt(jnp.mean(oh * oh, -1, keepdims=True) + RMS_EPS) * ng)
    return jnp.concatenate(outs, axis=-1) * (g * jax.nn.sigmoid(g))


def _gla_scan(refs, o_ref, acc_ref, scr, ng_ref, st_ref, n_chunks, group, s0):
    g_ref = refs[3]
    n_rows = group * GLA_CHUNK
    n_blocks = n_chunks // group
    half = n_blocks // 2

    def step(i, finish):
        bases = [pl.multiple_of(blk * n_rows, n_rows) for blk in (i, n_blocks - 1 - i)]
        preps = [_gla_block_prep(d, refs, scr, bases[d], _aligned(s0 + bases[d], n_rows), group) for d in (0, 1)]
        for d in (0, 1):
            rows = pl.ds(bases[d], n_rows)
            o = _gla_block_scan(d, preps[d], st_ref, group)
            if finish:
                o_ref[rows, :] = _gla_finish(acc_ref[rows, :] + o, g_ref[rows, :], ng_ref[...]).astype(bf16)
            else:
                acc_ref[rows, :] = o

    def first(i, carry):
        step(i, False)
        return carry

    def second(i, carry):
        step(i, True)
        return carry

    lax.fori_loop(0, half, first, 0)
    lax.fori_loop(half, n_blocks, second, 0)


def _gla_kernel(*refs):
    lat, ctx = refs[0:5], refs[5:10]
    gup_ref, gb_ref, ng_ref, o_ref, oc_ref, st_ref, acc_ref = refs[10:17]
    scr = refs[17:]
    st_ref[...] = jnp.zeros_like(st_ref)

    def decay_tiles(t, carry):
        rb = pl.multiple_of(t * GLA_PRE_TILE, GLA_PRE_TILE)
        for d in (0, 1):
            _gla_decay_tile(d, lat, gup_ref, gb_ref, scr, rb, rb)
        return carry

    for d in (0, 1):
        _gla_decay_tile(d, ctx, gup_ref, gb_ref, scr, 0, SEQ)
    lax.fori_loop(0, SEQ // GLA_PRE_TILE, decay_tiles, 0)
    _gla_scan(ctx, oc_ref, acc_ref, scr, ng_ref, st_ref, CTX_LEN // GLA_CHUNK, GLA_GROUP_CTX, SEQ)
    _gla_scan(lat, o_ref, acc_ref, scr, ng_ref, st_ref, SEQ // GLA_CHUNK, GLA_GROUP_LAT, 0)


def _gla(p, gup, gb, ng):
    def pieces(latent):
        ln = SEQ if latent else CTX_LEN
        if latent:
            blk = lambda width, off: pl.BlockSpec((None, ln, width), lambda b: (b, 0, off // width))
        else:
            blk = lambda width, off: pl.BlockSpec((None, ln, width), lambda b: (BATCH, b, off // width))
        return [blk(GLA_QK, EVEN_Q), blk(GLA_QK, EVEN_K), blk(GLA_WIDTH, EVEN_V), blk(GLA_WIDTH, EVEN_G),
                blk(LANE, EVEN_LO)]

    out = lambda ln: pl.BlockSpec((None, ln, GLA_WIDTH), lambda b: (b, 0, 0))
    all_rows = SEQ + CTX_LEN
    return pl.pallas_call(
        _gla_kernel,
        grid=(BATCH,),
        in_specs=pieces(True) + pieces(False) + [_resident(gup.shape), _resident(gb.shape), _resident(ng.shape)],
        out_specs=[out(SEQ), out(CTX_LEN)],
        out_shape=[jax.ShapeDtypeStruct((NB, SEQ, GLA_WIDTH), bf16),
                   jax.ShapeDtypeStruct((BATCH, CTX_LEN, GLA_WIDTH), bf16)],
        scratch_shapes=[pltpu.VMEM((2, GLA_HEADS, GLA_DV, LANE), f32), pltpu.VMEM((SEQ, GLA_WIDTH), f32),
                        pltpu.VMEM((2, all_rows, GLA_QK), bf16), pltpu.VMEM((2, all_rows, GLA_QK), bf16),
                        pltpu.VMEM((2, all_rows, GLA_QK), bf16),
                        pltpu.VMEM((2, all_rows // GLA_CHUNK * SUBLANE, GLA_QK), f32)],
        compiler_params=_cparams("parallel"),
        name="gla",
    )(*([p] * 10), gup, gb, ng)


def _even_w_in_aligned(w_in):
    main, lo = w_in[:, :EVEN_LO], w_in[:, EVEN_LO:]
    return jnp.concatenate([main, jnp.pad(lo, ((0, 0), (0, LANE - 2 * GLA_RANK)))], -1).astype(bf16)


def _gla_gate_params(gate_up, gate_b):
    ups = [jnp.zeros((LANE, GLA_QK), f32).at[j * GLA_RANK:(j + 1) * GLA_RANK].set(gate_up[j]) for j in range(2)]
    return jnp.stack(ups).astype(bf16), gate_b.reshape(2, 1, GLA_QK)


_HEAD_PERM = np.concatenate([np.arange(0, 32), np.arange(64, 96), np.arange(32, 64), np.arange(96, 128)])


def _attn_w_in_permuted(w_in):
    nqk = ATTN_Q_HEADS + ATTN_KV_HEADS
    cols = (np.arange(nqk)[:, None] * ATTN_HEAD_DIM + _HEAD_PERM[None, :]).reshape(-1)
    cols = np.concatenate([cols, np.arange(nqk * ATTN_HEAD_DIM, nqk * ATTN_HEAD_DIM + ATTN_KV_WIDTH)])
    return w_in[:, cols].astype(bf16)


def _rope_tables():
    rows = SEQ // GRID_W
    row = jnp.repeat(jnp.arange(rows, dtype=f32), GRID_W)
    col = jnp.tile(jnp.arange(GRID_W, dtype=f32), rows)
    inv_freq = ROPE_THETA ** (-jnp.arange(ROPE_AXIS_PAIRS, dtype=f32) / ROPE_AXIS_PAIRS)
    ang_r = row[:, None] * inv_freq
    ang_c = col[:, None] * inv_freq
    cos = jnp.concatenate([jnp.cos(ang_r), jnp.cos(ang_c), jnp.cos(ang_r), jnp.cos(ang_c)], -1)
    sin = jnp.concatenate([-jnp.sin(ang_r), -jnp.sin(ang_c), jnp.sin(ang_r), jnp.sin(ang_c)], -1)
    return cos, sin


def kernel(x, c, ctx, c_ctx, ada_w, ada_b, ln_g, ln_b, ffn_w1, ffn_w2, even_w_in, even_w_out, hyena_conv_w, hyena_conv_b, hyena_f_w1, hyena_f_b1, hyena_f_fr1, hyena_f_w2, hyena_f_b2, hyena_f_fr2, hyena_f_w3, hyena_f_b3, hyena_skip, gla_gate_up, gla_gate_b, gla_norm_g, attn_w_in, attn_w_out, attn_q_norm, attn_k_norm):
    c_all = jnp.concatenate([c, c_ctx[None], jnp.zeros((MOD_ROWS - NB, D_MODEL), f32)], 0)
    mods = _adaln(c_all, ada_w, ada_b).reshape(DEPTH, MOD_ROWS, N_MOD, D_MODEL)
    cos, sin = _rope_tables()
    dft_lat = _dft_tables(SEQ)
    dft_ctx = _dft_tables(CTX_LEN)
    w1, w2 = ffn_w1, ffn_w2
    lng = ln_g.reshape(DEPTH, 3, 1, D_MODEL)
    lnb = ln_b.reshape(DEPTH, 3, 1, D_MODEL)

    h = jnp.concatenate([x, ctx.reshape(1, SEQ, D_MODEL)], 0)
    for i in range(DEPTH):
        need_ctx = i < DEPTH - 1
        h = _ffn(h, mods, i, 0, 0, w1, w2, lng, lnb, NB)
        if i % 2 == 0:
            e = i // 2
            w_out = even_w_out[e].astype(bf16)
            gup, gb = _gla_gate_params(gla_gate_up[e], gla_gate_b[e])
            p = _proj(h, mods, i, _even_w_in_aligned(even_w_in[e]))
            filt = (hyena_f_w1[e], hyena_f_b1[e], hyena_f_fr1[e], hyena_f_w2[e], hyena_f_b2[e], hyena_f_fr2[e],
                    hyena_f_w3[e], hyena_f_b3[e])
            conv_b = hyena_conv_b[e].reshape(1, HYENA_PROJ)
            skip = hyena_skip[e].reshape(1, HYENA_WIDTH)
            kre, kim = _hyena_filter(SEQ, *filt, dft_lat[0], dft_lat[1])
            yh = _hyena(p, SEQ, hyena_conv_w[e], conv_b, skip, kre, kim, *dft_lat)
            kre_c, kim_c = _hyena_filter(CTX_LEN, *filt, dft_ctx[0], dft_ctx[1])
            yhc = _hyena(p, CTX_LEN, hyena_conv_w[e], conv_b, skip, kre_c, kim_c, *dft_ctx)
            yg, ygc = _gla(p, gup, gb, gla_norm_g[e].reshape(1, GLA_DV))
            xs = [_with_ctx_slab(yh, yhc), _with_ctx_slab(yg, ygc)]
        else:
            o = i // 2
            w_out = attn_w_out[o].astype(bf16)
            qn = attn_q_norm[o][_HEAD_PERM].reshape(1, LANE)
            kn = attn_k_norm[o][_HEAD_PERM].reshape(1, LANE)
            q, k, v = _attn_proj(h, mods, i, _attn_w_in_permuted(attn_w_in[o]), qn, kn, cos, sin)
            att = _attention(q, k, v, True)
            xs = [_with_ctx_slab(att, _attention(q, k, v, False)) if need_ctx else att]
        h = _mix_ffn(xs, w_out, h, mods, i, w1, w2, lng, lnb, NB if need_ctx else BATCH)
    return h
```

```python
import functools
import math

import jax
import jax.numpy as jnp
import numpy as np
from jax import lax
from jax.experimental import pallas as pl
from jax.experimental.pallas import tpu as pltpu

f32 = jnp.float32
bf16 = jnp.bfloat16

D_MODEL = 1024
BATCH = 8
SEQ = 2048
DEPTH = 4
GRID_W = 64
CTX_LEN = 256
N_MOD = 9
DEEPNORM_ALPHA = (2.0 * DEPTH) ** 0.25
LN_EPS = 1e-5
RMS_EPS = 1e-6
MACARON_WEIGHT = 0.5
FFN_HIDDEN = 2816

HYENA_WIDTH = 512
HYENA_PROJ = 3 * HYENA_WIDTH
HYENA_BANDS = 16
HYENA_EMB = 1 + 2 * HYENA_BANDS
HYENA_FILTER_HIDDEN = 64
HYENA_FAST_DECAY = 0.3
HYENA_SLOW_DECAY = 1.5
HYENA_TARGET = 1e-2

GLA_WIDTH = 512
GLA_HEADS = 4
GLA_DV = 128
GLA_DK = 64
GLA_QK = GLA_HEADS * GLA_DK
GLA_RANK = 16
GLA_TAU = 16.0
GLA_CHUNK = 64

ATTN_HEAD_DIM = 128
ATTN_Q_HEADS = 8
ATTN_KV_HEADS = 2
ATTN_GROUP = 4
ATTN_Q_WIDTH = 1024
ATTN_KV_WIDTH = 256
ROPE_THETA = 10000.0
ROPE_AXIS_PAIRS = 32

assert BATCH * CTX_LEN == SEQ
NB = BATCH + 1

LANE = 128
SUBLANE = 8
MOD_ROWS = 16
EMB_PAD = 128
VMEM_LIMIT = 56 * 1024 * 1024
TOKEN_TILE = 512

EVEN_Q = HYENA_PROJ
EVEN_K = EVEN_Q + GLA_QK
EVEN_V = EVEN_K + GLA_QK
EVEN_G = EVEN_V + GLA_WIDTH
EVEN_LO = EVEN_G + GLA_WIDTH
EVEN_W = EVEN_LO + LANE
HY_CBLK = 256
HY_FT = 512
V_EXT = 2 * LANE


def _cparams(*sem):
    return pltpu.CompilerParams(dimension_semantics=sem, vmem_limit_bytes=VMEM_LIMIT)


def _resident(shape, index=None):
    index = tuple(index or ())
    block = (None,) * len(index) + tuple(shape[len(index):])
    tail = (0,) * (len(shape) - len(index))
    return pl.BlockSpec(block, lambda *_: index + tail, pipeline_mode=pl.Buffered(1))


def _dot(a, b):
    return jnp.dot(a, b, preferred_element_type=f32)


def _dot_nt(a, b):
    return lax.dot_general(a, b, (((1,), (1,)), ((), ())), preferred_element_type=f32)


def _dot_tn(a, b):
    return lax.dot_general(a, b, (((0,), (0,)), ((), ())), preferred_element_type=f32)


def _split3(x):
    hi = x.astype(bf16)
    r1 = x - hi.astype(f32)
    mid = r1.astype(bf16)
    lo = (r1 - mid.astype(f32)).astype(bf16)
    return hi, mid, lo


def _dot_f32(a, b):
    a0, a1, a2 = _split3(a)
    b0, b1, b2 = _split3(b)
    return (_dot(a0, b0) + (_dot(a0, b1) + _dot(a1, b0))
            + (_dot(a0, b2) + _dot(a1, b1) + _dot(a2, b0)))


def _layer_norm(x, g, b):
    mu = jnp.mean(x, -1, keepdims=True)
    xc = x - mu
    var = jnp.mean(xc * xc, -1, keepdims=True)
    return xc * lax.rsqrt(var + LN_EPS) * g + b


def _mod_rows(mod_ref, slot):
    return (mod_ref[3 * slot:3 * slot + 1, :], mod_ref[3 * slot + 1:3 * slot + 2, :],
            mod_ref[3 * slot + 2:3 * slot + 3, :])


def _mod_spec(layer):
    return pl.BlockSpec((None, None, N_MOD, D_MODEL), lambda b, *_: (layer, b, 0, 0))


def _tok_spec(width):
    return pl.BlockSpec((None, TOKEN_TILE, width), lambda b, t: (b, t, 0))


def _ln_specs(layer, slot):
    return [_resident((DEPTH, 3, 1, D_MODEL), (layer, slot))] * 2


def _adaln_kernel(c_ref, w_ref, b_ref, o_ref):
    c = c_ref[...]
    sc = (c * jax.nn.sigmoid(c)).astype(bf16)
    o_ref[...] = _dot(sc, w_ref[...].astype(bf16)) + b_ref[...]


def _adaln(c_all, ada_w, ada_b):
    return pl.pallas_call(
        _adaln_kernel,
        grid=(DEPTH, N_MOD),
        in_specs=[
            pl.BlockSpec((MOD_ROWS, D_MODEL), lambda l, j: (0, 0)),
            pl.BlockSpec((None, D_MODEL, D_MODEL), lambda l, j: (l, 0, j)),
            pl.BlockSpec((None, 1, D_MODEL), lambda l, j: (l, 0, j)),
        ],
        out_specs=pl.BlockSpec((None, MOD_ROWS, D_MODEL), lambda l, j: (l, 0, j)),
        out_shape=jax.ShapeDtypeStruct((DEPTH, MOD_ROWS, N_MOD * D_MODEL), f32),
        compiler_params=_cparams("parallel", "parallel"),
        name="adaln",
    )(c_all, ada_w, ada_b.reshape(DEPTH, 1, N_MOD * D_MODEL))


FFN_TH = 256


def _ffn_sublayer(h, mod_ref, slot, w1_ref, w2_ref, g_ref, b_ref):
    shift, scale, gate = _mod_rows(mod_ref, slot)
    a = (h * (1.0 + scale) + shift).astype(bf16)
    acc = None
    for lo in range(0, FFN_HIDDEN, FFN_TH):
        gt = _dot(a, w1_ref[:, lo:lo + FFN_TH].astype(bf16))
        up = _dot(a, w1_ref[:, FFN_HIDDEN + lo:FFN_HIDDEN + lo + FFN_TH].astype(bf16))
        act = (gt * jax.nn.sigmoid(gt) * up).astype(bf16)
        part = _dot(act, w2_ref[lo:lo + FFN_TH, :].astype(bf16))
        acc = part if acc is None else acc + part
    return _layer_norm(DEEPNORM_ALPHA * h + gate * (MACARON_WEIGHT * acc), g_ref[...], b_ref[...])


def _ffn_kernel(slot, split_input, *refs):
    if split_input:
        x_ref, c_ref, mod_ref, w1_ref, w2_ref, g_ref, b_ref, o_ref = refs
        h = jnp.where(pl.program_id(0) == BATCH, c_ref[...], x_ref[...])
    else:
        h_ref, mod_ref, w1_ref, w2_ref, g_ref, b_ref, o_ref = refs
        h = h_ref[...]
    o_ref[...] = _ffn_sublayer(h, mod_ref, slot, w1_ref, w2_ref, g_ref, b_ref)


def _ffn(h, mods, layer, slot, which, w1, w2, ln_g, ln_b, n_slabs):
    split_input = isinstance(h, tuple)
    if split_input:
        tok_specs = [pl.BlockSpec((None, TOKEN_TILE, D_MODEL), lambda b, t: (jnp.minimum(b, BATCH - 1), t, 0)),
                     pl.BlockSpec((None, TOKEN_TILE, D_MODEL), lambda b, t: (0, t, 0))]
        hs = h
    else:
        tok_specs, hs = [_tok_spec(D_MODEL)], (h,)
    return pl.pallas_call(
        functools.partial(_ffn_kernel, slot, split_input),
        grid=(n_slabs, SEQ // TOKEN_TILE),
        in_specs=tok_specs + [_mod_spec(layer), _resident(w1.shape, (layer, which)),
                              _resident(w2.shape, (layer, which)), *_ln_specs(layer, slot)],
        out_specs=_tok_spec(D_MODEL),
        out_shape=jax.ShapeDtypeStruct((n_slabs, SEQ, D_MODEL), f32),
        compiler_params=_cparams("parallel", "parallel"),
        name="ffn",
    )(*hs, mods, w1, w2, ln_g, ln_b)


def _mix_ffn_kernel(n_in, *refs):
    x_refs, wo_refs = refs[:n_in], refs[n_in:2 * n_in]
    h_ref, mod_ref, g1_ref, b1_ref, w1_ref, w2_ref, g2_ref, b2_ref, o_ref = refs[2 * n_in:]
    _, _, gate = _mod_rows(mod_ref, 1)
    y = None
    for x_ref, wo_ref in zip(x_refs, wo_refs):
        part = _dot(x_ref[...], wo_ref[...])
        y = part if y is None else y + part
    h1 = _layer_norm(DEEPNORM_ALPHA * h_ref[...] + gate * y, g1_ref[...], b1_ref[...])
    o_ref[...] = _ffn_sublayer(h1, mod_ref, 2, w1_ref, w2_ref, g2_ref, b2_ref)


def _mix_ffn(xs, w_out, h, mods, layer, w1, w2, ln_g, ln_b, n_slabs):
    in_specs, row = [], 0
    for x in xs:
        in_specs.append(_tok_spec(x.shape[-1]))
    for x in xs:
        width = x.shape[-1]
        in_specs.append(pl.BlockSpec((width, D_MODEL), lambda b, t, r=row // width: (r, 0),
                                     pipeline_mode=pl.Buffered(1)))
        row += width
    return pl.pallas_call(
        functools.partial(_mix_ffn_kernel, len(xs)),
        grid=(n_slabs, SEQ // TOKEN_TILE),
        in_specs=in_specs + [_tok_spec(D_MODEL), _mod_spec(layer), *_ln_specs(layer, 1),
                             _resident(w1.shape, (layer, 1)), _resident(w2.shape, (layer, 1)), *_ln_specs(layer, 2)],
        out_specs=_tok_spec(D_MODEL),
        out_shape=jax.ShapeDtypeStruct((n_slabs, SEQ, D_MODEL), f32),
        compiler_params=_cparams("parallel", "parallel"),
        name="mix_ffn",
    )(*xs, *([w_out] * len(xs)), h, mods, ln_g, ln_b, w1, w2, ln_g, ln_b)


def _proj_kernel(tn, h_ref, mod_ref, w_ref, o_ref):
    shift, scale, _ = _mod_rows(mod_ref, 1)
    a = (h_ref[...] * (1.0 + scale) + shift).astype(bf16)
    n = w_ref.shape[1]
    for j in range(0, n, tn):
        w = min(tn, n - j)
        o_ref[:, j:j + w] = _dot(a, w_ref[:, j:j + w])


def _proj(h, mods, layer, w):
    n = w.shape[1]
    return pl.pallas_call(
        functools.partial(_proj_kernel, 512),
        grid=(NB, SEQ // TOKEN_TILE),
        in_specs=[_tok_spec(D_MODEL), _mod_spec(layer), _resident(w.shape)],
        out_specs=_tok_spec(n),
        out_shape=jax.ShapeDtypeStruct((NB, SEQ, n), f32),
        compiler_params=_cparams("parallel", "parallel"),
        name="even_proj",
    )(h, mods, w)


ATTN_PROJ_TILE = 256
Q_PRESCALE = ATTN_HEAD_DIM ** -0.5 * math.log2(math.e)


def _attn_proj_kernel(h_ref, mod_ref, w_ref, qn_ref, kn_ref, cos_ref, sin_ref, q_ref, k_ref, v_ref):
    shift, scale, _ = _mod_rows(mod_ref, 1)
    a = (h_ref[...] * (1.0 + scale) + shift).astype(bf16)
    is_ctx = pl.program_id(0) == BATCH
    cos = jnp.where(is_ctx, 1.0, cos_ref[...])
    sin = jnp.where(is_ctx, 0.0, sin_ref[...])

    def norm_rope(x, gain):
        x = x * lax.rsqrt(jnp.mean(x * x, -1, keepdims=True) + RMS_EPS) * gain
        return x * cos + pltpu.roll(x, ATTN_HEAD_DIM // 2, 1) * sin

    for half in range(2):
        x4 = _dot(a, w_ref[:, half * 512:(half + 1) * 512])
        for hh in range(4):
            hd = half * 4 + hh
            q_ref[:, hd * LANE:(hd + 1) * LANE] = (
                norm_rope(x4[:, hh * LANE:(hh + 1) * LANE], qn_ref[...]) * Q_PRESCALE).astype(bf16)
    kx = _dot(a, w_ref[:, ATTN_Q_WIDTH:ATTN_Q_WIDTH + ATTN_KV_WIDTH])
    for hd in range(ATTN_KV_HEADS):
        k_ref[:, hd * LANE:(hd + 1) * LANE] = norm_rope(kx[:, hd * LANE:(hd + 1) * LANE], kn_ref[...]).astype(bf16)
    vx = _dot(a, w_ref[:, ATTN_Q_WIDTH + ATTN_KV_WIDTH:ATTN_Q_WIDTH + 2 * ATTN_KV_WIDTH])
    ones = jnp.ones((a.shape[0], LANE), bf16)
    for hd in range(ATTN_KV_HEADS):
        v_ref[:, hd * V_EXT:hd * V_EXT + LANE] = vx[:, hd * LANE:(hd + 1) * LANE].astype(bf16)
        v_ref[:, hd * V_EXT + LANE:(hd + 1) * V_EXT] = ones


def _attn_proj(h, mods, layer, w, qn, kn, cos, sin):
    tm = ATTN_PROJ_TILE
    tok = lambda width: pl.BlockSpec((None, tm, width), lambda b, t: (b, t, 0))
    rope = pl.BlockSpec((tm, LANE), lambda b, t: (t, 0))
    return pl.pallas_call(
        _attn_proj_kernel,
        grid=(NB, SEQ // tm),
        in_specs=[tok(D_MODEL), _mod_spec(layer), _resident(w.shape), _resident((1, LANE)), _resident((1, LANE)),
                  rope, rope],
        out_specs=[tok(ATTN_Q_WIDTH), tok(ATTN_KV_WIDTH), tok(ATTN_KV_HEADS * V_EXT)],
        out_shape=[jax.ShapeDtypeStruct((NB, SEQ, ATTN_Q_WIDTH), bf16),
                   jax.ShapeDtypeStruct((NB, SEQ, ATTN_KV_WIDTH), bf16),
                   jax.ShapeDtypeStruct((NB, SEQ, ATTN_KV_HEADS * V_EXT), bf16)],
        compiler_params=_cparams("parallel", "parallel"),
        name="attn_proj",
    )(h, mods, w, qn, kn, cos, sin)


ATTN_TQ = 512


def _attn_kernel(has_lat, q_ref, kc_ref, vc_ref, *rest):
    if has_lat:
        k_ref, v_ref, o_ref = rest
    else:
        (o_ref,) = rest
    for g in range(ATTN_KV_HEADS):
        cols = slice(g * LANE, (g + 1) * LANE)
        vcols = slice(g * V_EXT, (g + 1) * V_EXT)
        for hh in range(ATTN_GROUP):
            hcols = slice((g * ATTN_GROUP + hh) * LANE, (g * ATTN_GROUP + hh + 1) * LANE)
            q = q_ref[:, hcols]
            sc = _dot_nt(q, kc_ref[:, cols])
            m = jnp.max(sc, -1, keepdims=True)
            if has_lat:
                s = _dot_nt(q, k_ref[:, cols])
                m = jnp.maximum(m, jnp.max(s, -1, keepdims=True))
                ov = _dot(jnp.exp2(s - m).astype(bf16), v_ref[:, vcols])
                ov = ov + _dot(jnp.exp2(sc - m).astype(bf16), vc_ref[:, vcols])
            else:
                ov = _dot(jnp.exp2(sc - m).astype(bf16), vc_ref[:, vcols])
            o_ref[:, hcols] = (ov[:, :LANE] / ov[:, LANE:]).astype(bf16)


def _attention(q, k, v, latent):
    tq = ATTN_TQ if latent else CTX_LEN
    vw = ATTN_KV_HEADS * V_EXT
    if latent:
        grid = (BATCH, SEQ // tq)
        qspec = pl.BlockSpec((None, tq, ATTN_Q_WIDTH), lambda b, t: (b, t, 0))
        ctx_kv = lambda width: pl.BlockSpec((None, CTX_LEN, width), lambda b, t: (BATCH, b, 0))
        lat_kv = lambda width: pl.BlockSpec((None, SEQ, width), lambda b, t: (b, 0, 0))
        in_specs = [qspec, ctx_kv(ATTN_KV_WIDTH), ctx_kv(vw), lat_kv(ATTN_KV_WIDTH), lat_kv(vw)]
        args = (q, k, v, k, v)
        out_shape = jax.ShapeDtypeStruct((NB, SEQ, ATTN_Q_WIDTH), bf16)
    else:
        grid = (BATCH, 1)
        qspec = pl.BlockSpec((None, tq, ATTN_Q_WIDTH), lambda b, t: (BATCH, b, 0))
        ctx_kv = lambda width: pl.BlockSpec((None, CTX_LEN, width), lambda b, t: (BATCH, b, 0))
        in_specs = [qspec, ctx_kv(ATTN_KV_WIDTH), ctx_kv(vw)]
        args = (q, k, v)
        out_shape = jax.ShapeDtypeStruct((BATCH, CTX_LEN, ATTN_Q_WIDTH), bf16)
    return pl.pallas_call(
        functools.partial(_attn_kernel, latent),
        grid=grid,
        in_specs=in_specs,
        out_specs=pl.BlockSpec((None, tq, ATTN_Q_WIDTH), lambda b, t: (b, t, 0)),
        out_shape=out_shape,
        compiler_params=_cparams("parallel", "parallel"),
        name="attention",
    )(*args)


def _with_ctx_slab(lat, ctx):
    return lat.at[BATCH].set(ctx.reshape(SEQ, ctx.shape[-1]))


def _dft_tables(length):
    idx = np.arange(length, dtype=np.int64)
    ang = (np.outer(idx, idx) % (2 * length)).astype(np.float64) * (math.pi / length)
    return tuple(jnp.asarray(t, dtype=f32).astype(bf16) for t in (np.cos(ang), -np.sin(ang)))


def _filter_kernel(length, feats_ref, t_ref, dl_ref, w1_ref, b1_ref, fr1_ref, w2_ref, b2_ref, fr2_ref,
                   w3_ref, b3_ref, cos_ref, sin_ref, kre_ref, kim_ref):
    h = jnp.sin(fr1_ref[...] * (_dot_f32(feats_ref[...], w1_ref[...]) + b1_ref[...]))
    h = jnp.sin(fr2_ref[...] * (_dot_f32(h, w2_ref[...]) + b2_ref[...]))
    h = _dot_f32(h, w3_ref[...]) + b3_ref[...]
    win = jnp.exp(-t_ref[...] * dl_ref[...])
    row = lax.broadcasted_iota(jnp.int32, (length, HYENA_WIDTH), 0)
    hf = h[:, :HYENA_WIDTH] * win
    hb = jnp.where(row == 0, 0.0, h[:, HYENA_WIDTH:] * win)
    inv = lax.rsqrt(jnp.sum(hf * hf + hb * hb, 0, keepdims=True) + 1e-6)
    even = (hf + hb) * inv
    odd = (hf - hb) * inv
    sign = jnp.where((row & 1) == 0, 1.0, -1.0)
    nyq = jnp.sum(even * sign, 0, keepdims=True)
    wgt = jnp.where(row == 0, 1.0, 2.0) * (0.5 / length)
    e0, e1, _ = _split3(even)
    o0, o1, _ = _split3(odd)
    cm, sm = cos_ref[...], sin_ref[...]
    kre = _dot(cm, e0) + _dot(cm, e1)
    kim = _dot(sm, o0) + _dot(sm, o1)
    kre_ref[...] = kre * wgt
    kim_ref[...] = jnp.where(row == 0, nyq, kim) * wgt


def _hyena_filter(length, w1, b1, fr1, w2, b2, fr2, w3, b3, cos_m, sin_m):
    n = np.arange(length, dtype=np.float32)[:, None]
    t = np.linspace(0.0, 1.0, length, dtype=np.float32)[:, None]
    bands = np.linspace(1e-4, HYENA_BANDS - 1, HYENA_BANDS, dtype=np.float32)[None, :]
    ang = (np.float32(2.0 * math.pi) * n * bands / np.float32(length)).astype(np.float32)
    feats = np.zeros((length, EMB_PAD), np.float32)
    feats[:, :HYENA_EMB] = np.concatenate([t, np.cos(ang), -np.sin(ang)], axis=-1)
    max_decay = math.log(HYENA_TARGET) / HYENA_FAST_DECAY
    min_decay = math.log(HYENA_TARGET) / HYENA_SLOW_DECAY
    deltas = np.abs(np.linspace(min_decay, max_decay, HYENA_WIDTH, dtype=np.float32))[None, :]
    w1p = jnp.zeros((EMB_PAD, HYENA_FILTER_HIDDEN), f32).at[:HYENA_EMB].set(w1)
    row = lambda v: v.reshape(1, -1)
    args = (jnp.asarray(feats), jnp.asarray(t), jnp.asarray(deltas), w1p, row(b1), row(fr1), w2, row(b2), row(fr2),
            w3, row(b3), cos_m, sin_m)
    out = jax.ShapeDtypeStruct((length, HYENA_WIDTH), f32)
    return pl.pallas_call(
        functools.partial(_filter_kernel, length),
        out_shape=[out, out],
        compiler_params=pltpu.CompilerParams(vmem_limit_bytes=VMEM_LIMIT),
        name="hyena_filter",
    )(*args)


def _conv3(p_ref, w_ref, b_ref, length):
    x = p_ref[...]
    row = lax.broadcasted_iota(jnp.int32, x.shape, 0)
    prev = jnp.where(row == 0, 0.0, pltpu.roll(x, 1, 0))
    nxt = jnp.where(row == length - 1, 0.0, pltpu.roll(x, length - 1, 0))
    return prev * w_ref[0:1, :] + x * w_ref[1:2, :] + nxt * w_ref[2:3, :] + b_ref[...]


def _hyena_kernel(length, p0_ref, p1_ref, p2_ref, w0_ref, w1_ref, w2_ref, b0_ref, b1_ref, b2_ref, skip_ref,
                  kre_ref, kim_ref, cos_ref, sin_ref, o_ref, yre_ref, yim_ref):
    z = _conv3(p0_ref, w0_ref, b0_ref, length) * _conv3(p1_ref, w1_ref, b1_ref, length)
    zb = z.astype(bf16)
    ft = min(length, HY_FT)
    for f0 in range(0, length, ft):
        rows = slice(f0, f0 + ft)
        zre = _dot(cos_ref[rows, :], zb)
        zim = _dot(sin_ref[rows, :], zb)
        kre, kim = kre_ref[rows, :], kim_ref[rows, :]
        yre_ref[rows, :] = (zre * kre - zim * kim).astype(bf16)
        yim_ref[rows, :] = (zre * kim + zim * kre).astype(bf16)
    row = lax.broadcasted_iota(jnp.int32, z.shape, 0)
    sign = jnp.where((row & 1) == 0, 1.0, -1.0)
    y_nyq = jnp.sum(z * sign, 0, keepdims=True) * kim_ref[0:1, :]
    y = _dot(cos_ref[...], yre_ref[...]) + _dot(sin_ref[...], yim_ref[...]) + sign * y_nyq
    o_ref[...] = ((y + z * skip_ref[...]) * _conv3(p2_ref, w2_ref, b2_ref, length)).astype(bf16)


def _hyena(p, length, conv_w, conv_b, skip, kre, kim, cos_m, sin_m):
    nblk = HYENA_WIDTH // HY_CBLK
    n_out = NB if length == SEQ else BATCH
    if length == SEQ:
        part = lambda j: pl.BlockSpec((None, length, HY_CBLK), lambda c, b: (b, 0, j * nblk + c))
    else:
        part = lambda j: pl.BlockSpec((None, length, HY_CBLK), lambda c, b: (BATCH, b, j * nblk + c))
    wpart = lambda j, rows: pl.BlockSpec((rows, HY_CBLK), lambda c, b: (0, j * nblk + c))
    chan = pl.BlockSpec((1, HY_CBLK), lambda c, b: (0, c))
    spec_k = pl.BlockSpec((length, HY_CBLK), lambda c, b: (0, c), pipeline_mode=pl.Buffered(1))
    table = _resident((length, length))
    return pl.pallas_call(
        functools.partial(_hyena_kernel, length),
        grid=(nblk, BATCH),
        in_specs=[part(0), part(1), part(2), wpart(0, 3), wpart(1, 3), wpart(2, 3), wpart(0, 1), wpart(1, 1),
                  wpart(2, 1), chan, spec_k, spec_k, table, table],
        out_specs=pl.BlockSpec((None, length, HY_CBLK), lambda c, b: (b, 0, c)),
        out_shape=jax.ShapeDtypeStruct((n_out, length, HYENA_WIDTH), bf16),
        scratch_shapes=[pltpu.VMEM((length, HY_CBLK), bf16), pltpu.VMEM((length, HY_CBLK), bf16)],
        compiler_params=_cparams("parallel", "parallel"),
        name="hyena",
    )(p, p, p, conv_w, conv_w, conv_w, conv_b, conv_b, conv_b, skip, kre, kim, cos_m, sin_m)


GLA_GROUP_LAT = 4
GLA_GROUP_CTX = 2
GLA_HEADS_PER_TILE = LANE // GLA_DK
GLA_PRE_TILE = 256


def _log_sigmoid(x):
    return jnp.minimum(x, 0.0) - jnp.log(1.0 + jnp.exp(-jnp.abs(x)))


def _head_tile(hd):
    tile = hd // GLA_HEADS_PER_TILE
    lane = lax.broadcasted_iota(jnp.int32, (1, LANE), 1)
    first = (hd % GLA_HEADS_PER_TILE) * GLA_DK
    return slice(tile * LANE, (tile + 1) * LANE), (lane >= first) & (lane < first + GLA_DK)


def _aligned(x, m):
    return x if isinstance(x, int) else pl.multiple_of(x, m)


def _chunk_tri(n_rows, direction):
    ri = lax.broadcasted_iota(jnp.int32, (n_rows, n_rows), 0)
    ci = lax.broadcasted_iota(jnp.int32, (n_rows, n_rows), 1)
    same_chunk = (ri // GLA_CHUNK) == (ci // GLA_CHUNK)
    return same_chunk & ((ci <= ri) if direction == 0 else (ci >= ri))


def _gla_decay_tile(direction, refs, gup_ref, gb_ref, scr, rb, sb):
    q_ref, k_ref, _, _, lo_ref = refs
    qin_ref, kin_ref, kst_ref, dec_ref = scr
    group = GLA_PRE_TILE // GLA_CHUNK
    rows = pl.ds(rb, GLA_PRE_TILE)
    tri_b = jnp.where(_chunk_tri(GLA_PRE_TILE, direction), 1.0, 0.0).astype(bf16)
    x = _dot(lo_ref[rows, :].astype(bf16), gup_ref[direction]) + gb_ref[direction]
    lg = _log_sigmoid(x) * (1.0 / GLA_TAU)
    l0, l1, _ = _split3(lg)
    cum = _dot(tri_b, l0) + _dot(tri_b, l1)
    tots = []
    for g in range(group):
        end = g * GLA_CHUNK + (GLA_CHUNK - 1 if direction == 0 else 0)
        tots.append(cum[end:end + 1, :])
    tot_rows = jnp.concatenate([jnp.broadcast_to(t, (GLA_CHUNK, t.shape[1])) for t in tots], 0)
    k = k_ref[rows, :]
    srows = pl.ds(sb, GLA_PRE_TILE)
    qin_ref[direction, srows, :] = (q_ref[rows, :] * (GLA_DK ** -0.5) * jnp.exp(cum)).astype(bf16)
    kin_ref[direction, srows, :] = (k * jnp.exp(-cum)).astype(bf16)
    kst_ref[direction, srows, :] = (k * jnp.exp(tot_rows - cum)).astype(bf16)
    for g in range(group):
        drow = _aligned((sb // GLA_CHUNK + g) * SUBLANE, SUBLANE)
        dec_ref[direction, pl.ds(drow, SUBLANE), :] = jnp.broadcast_to(jnp.exp(tots[g]), (SUBLANE, GLA_QK))


def _gla_block_prep(direction, refs, scr, rb, sb, group):
    v_ref = refs[2]
    qin_ref, kin_ref, kst_ref, dec_ref = scr
    n_rows = group * GLA_CHUNK
    tri = _chunk_tri(n_rows, direction)
    srows = pl.ds(sb, n_rows)
    q_in, k_in, k_st = qin_ref[direction, srows, :], kin_ref[direction, srows, :], kst_ref[direction, srows, :]
    dec = dec_ref[direction, pl.ds(_aligned(sb // GLA_CHUNK * SUBLANE, group * SUBLANE), group * SUBLANE), :]
    decs = [dec[g * SUBLANE:g * SUBLANE + 1, :] for g in range(group)]
    v = v_ref[pl.ds(rb, n_rows), :].astype(bf16)
    q_heads, intra, ds_t = [], [], [[] for _ in range(group)]
    for hd in range(GLA_HEADS):
        cols, lanes = _head_tile(hd)
        vc = slice(hd * GLA_DV, (hd + 1) * GLA_DV)
        q_h = jnp.where(lanes, q_in[:, cols], 0.0).astype(bf16)
        kst_h = jnp.where(lanes, k_st[:, cols], 0.0).astype(bf16)
        att = jnp.where(tri, _dot_nt(q_h, k_in[:, cols]), 0.0)
        q_heads.append(q_h)
        intra.append(_dot(att.astype(bf16), v[:, vc]))
        for g in range(group):
            cr = slice(g * GLA_CHUNK, (g + 1) * GLA_CHUNK)
            ds_t[g].append(_dot_tn(v[cr, vc], kst_h[cr, :]))
    return q_heads, intra, ds_t, decs


def _gla_block_scan(direction, prep, st_ref, group):
    q_heads, intra, ds_t, decs = prep
    order = range(group) if direction == 0 else range(group - 1, -1, -1)
    outs = []
    for hd in range(GLA_HEADS):
        cols, _ = _head_tile(hd)
        st = st_ref[direction, hd]
        inter = [None] * group
        for g in order:
            cr = slice(g * GLA_CHUNK, (g + 1) * GLA_CHUNK)
            inter[g] = _dot_nt(q_heads[hd][cr, :], st.astype(bf16))
            st = st * decs[g][:, cols] + ds_t[g][hd]
        st_ref[direction, hd] = st
        outs.append(intra[hd] + jnp.concatenate(inter, 0))
    return jnp.concatenate(outs, axis=-1)


def _gla_finish(o, g, ng):
    outs = []
    for hd in range(GLA_HEADS):
        oh = o[:, hd * GLA_DV:(hd + 1) * GLA_DV]
        outs.append(oh * lax.rsqrt(jnp.mean(oh * oh, -1, keepdims=True) + RMS_EPS) * ng)
    return jnp.concatenate(outs, axis=-1) * (g * jax.nn.sigmoid(g))


def _gla_scan(refs, o_ref, acc_ref, scr, ng_ref, st_ref, n_chunks, group, s0):
    g_ref = refs[3]
    n_rows = group * GLA_CHUNK
    n_blocks = n_chunks // group
    half = n_blocks // 2

    def step(i, finish):
        bases = [pl.multiple_of(blk * n_rows, n_rows) for blk in (i, n_blocks - 1 - i)]
        preps = [_gla_block_prep(d, refs, scr, bases[d], _aligned(s0 + bases[d], n_rows), group) for d in (0, 1)]
        for d in (0, 1):
            rows = pl.ds(bases[d], n_rows)
            o = _gla_block_scan(d, preps[d], st_ref, group)
            if finish:
                o_ref[rows, :] = _gla_finish(acc_ref[rows, :] + o, g_ref[rows, :], ng_ref[...]).astype(bf16)
            else:
                acc_ref[rows, :] = o

    def first(i, carry):
        step(i, False)
        return carry

    def second(i, carry):
        step(i, True)
        return carry

    lax.fori_loop(0, half, first, 0)
    lax.fori_loop(half, n_blocks, second, 0)


def _gla_kernel(*refs):
    lat, ctx = refs[0:5], refs[5:10]
    gup_ref, gb_ref, ng_ref, o_ref, oc_ref, st_ref, acc_ref = refs[10:17]
    scr = refs[17:]
    st_ref[...] = jnp.zeros_like(st_ref)

    def decay_tiles(t, carry):
        rb = pl.multiple_of(t * GLA_PRE_TILE, GLA_PRE_TILE)
        for d in (0, 1):
            _gla_decay_tile(d, lat, gup_ref, gb_ref, scr, rb, rb)
        return carry

    for d in (0, 1):
        _gla_decay_tile(d, ctx, gup_ref, gb_ref, scr, 0, SEQ)
    lax.fori_loop(0, SEQ // GLA_PRE_TILE, decay_tiles, 0)
    _gla_scan(ctx, oc_ref, acc_ref, scr, ng_ref, st_ref, CTX_LEN // GLA_CHUNK, GLA_GROUP_CTX, SEQ)
    _gla_scan(lat, o_ref, acc_ref, scr, ng_ref, st_ref, SEQ // GLA_CHUNK, GLA_GROUP_LAT, 0)


def _gla(p, gup, gb, ng):
    def pieces(latent):
        ln = SEQ if latent else CTX_LEN
        if latent:
            blk = lambda width, off: pl.BlockSpec((None, ln, width), lambda b: (b, 0, off // width))
        else:
            blk = lambda width, off: pl.BlockSpec((None, ln, width), lambda b: (BATCH, b, off // width))
        return [blk(GLA_QK, EVEN_Q), blk(GLA_QK, EVEN_K), blk(GLA_WIDTH, EVEN_V), blk(GLA_WIDTH, EVEN_G),
                blk(LANE, EVEN_LO)]

    out = lambda ln: pl.BlockSpec((None, ln, GLA_WIDTH), lambda b: (b, 0, 0))
    all_rows = SEQ + CTX_LEN
    return pl.pallas_call(
        _gla_kernel,
        grid=(BATCH,),
        in_specs=pieces(True) + pieces(False) + [_resident(gup.shape), _resident(gb.shape), _resident(ng.shape)],
        out_specs=[out(SEQ), out(CTX_LEN)],
        out_shape=[jax.ShapeDtypeStruct((NB, SEQ, GLA_WIDTH), bf16),
                   jax.ShapeDtypeStruct((BATCH, CTX_LEN, GLA_WIDTH), bf16)],
        scratch_shapes=[pltpu.VMEM((2, GLA_HEADS, GLA_DV, LANE), f32), pltpu.VMEM((SEQ, GLA_WIDTH), f32),
                        pltpu.VMEM((2, all_rows, GLA_QK), bf16), pltpu.VMEM((2, all_rows, GLA_QK), bf16),
                        pltpu.VMEM((2, all_rows, GLA_QK), bf16),
                        pltpu.VMEM((2, all_rows // GLA_CHUNK * SUBLANE, GLA_QK), f32)],
        compiler_params=_cparams("parallel"),
        name="gla",
    )(*([p] * 10), gup, gb, ng)


def _even_w_in_aligned(w_in):
    main, lo = w_in[:, :EVEN_LO], w_in[:, EVEN_LO:]
    return jnp.concatenate([main, jnp.pad(lo, ((0, 0), (0, LANE - 2 * GLA_RANK)))], -1).astype(bf16)


def _gla_gate_params(gate_up, gate_b):
    ups = [jnp.zeros((LANE, GLA_QK), f32).at[j * GLA_RANK:(j + 1) * GLA_RANK].set(gate_up[j]) for j in range(2)]
    return jnp.stack(ups).astype(bf16), gate_b.reshape(2, 1, GLA_QK)


_HEAD_PERM = np.concatenate([np.arange(0, 32), np.arange(64, 96), np.arange(32, 64), np.arange(96, 128)])


def _attn_w_in_permuted(w_in):
    nqk = ATTN_Q_HEADS + ATTN_KV_HEADS
    cols = (np.arange(nqk)[:, None] * ATTN_HEAD_DIM + _HEAD_PERM[None, :]).reshape(-1)
    cols = np.concatenate([cols, np.arange(nqk * ATTN_HEAD_DIM, nqk * ATTN_HEAD_DIM + ATTN_KV_WIDTH)])
    return w_in[:, cols].astype(bf16)


def _rope_tables():
    rows = SEQ // GRID_W
    row = jnp.repeat(jnp.arange(rows, dtype=f32), GRID_W)
    col = jnp.tile(jnp.arange(GRID_W, dtype=f32), rows)
    inv_freq = ROPE_THETA ** (-jnp.arange(ROPE_AXIS_PAIRS, dtype=f32) / ROPE_AXIS_PAIRS)
    ang_r = row[:, None] * inv_freq
    ang_c = col[:, None] * inv_freq
    cos = jnp.concatenate([jnp.cos(ang_r), jnp.cos(ang_c), jnp.cos(ang_r), jnp.cos(ang_c)], -1)
    sin = jnp.concatenate([-jnp.sin(ang_r), -jnp.sin(ang_c), jnp.sin(ang_r), jnp.sin(ang_c)], -1)
    return cos, sin


def kernel(x, c, ctx, c_ctx, ada_w, ada_b, ln_g, ln_b, ffn_w1, ffn_w2, even_w_in, even_w_out, hyena_conv_w, hyena_conv_b, hyena_f_w1, hyena_f_b1, hyena_f_fr1, hyena_f_w2, hyena_f_b2, hyena_f_fr2, hyena_f_w3, hyena_f_b3, hyena_skip, gla_gate_up, gla_gate_b, gla_norm_g, attn_w_in, attn_w_out, attn_q_norm, attn_k_norm):
    c_all = jnp.concatenate([c, c_ctx[None], jnp.zeros((MOD_ROWS - NB, D_MODEL), f32)], 0)
    mods = _adaln(c_all, ada_w, ada_b).reshape(DEPTH, MOD_ROWS, N_MOD, D_MODEL)
    cos, sin = _rope_tables()
    dft_lat = _dft_tables(SEQ)
    dft_ctx = _dft_tables(CTX_LEN)
    w1, w2 = ffn_w1, ffn_w2
    lng = ln_g.reshape(DEPTH, 3, 1, D_MODEL)
    lnb = ln_b.reshape(DEPTH, 3, 1, D_MODEL)

    h = (x, ctx.reshape(1, SEQ, D_MODEL))
    for i in range(DEPTH):
        need_ctx = i < DEPTH - 1
        h = _ffn(h, mods, i, 0, 0, w1, w2, lng, lnb, NB)
        if i % 2 == 0:
            e = i // 2
            w_out = even_w_out[e].astype(bf16)
            gup, gb = _gla_gate_params(gla_gate_up[e], gla_gate_b[e])
            p = _proj(h, mods, i, _even_w_in_aligned(even_w_in[e]))
            filt = (hyena_f_w1[e], hyena_f_b1[e], hyena_f_fr1[e], hyena_f_w2[e], hyena_f_b2[e], hyena_f_fr2[e],
                    hyena_f_w3[e], hyena_f_b3[e])
            conv_b = hyena_conv_b[e].reshape(1, HYENA_PROJ)
            skip = hyena_skip[e].reshape(1, HYENA_WIDTH)
            kre, kim = _hyena_filter(SEQ, *filt, dft_lat[0], dft_lat[1])
            yh = _hyena(p, SEQ, hyena_conv_w[e], conv_b, skip, kre, kim, *dft_lat)
            kre_c, kim_c = _hyena_filter(CTX_LEN, *filt, dft_ctx[0], dft_ctx[1])
            yhc = _hyena(p, CTX_LEN, hyena_conv_w[e], conv_b, skip, kre_c, kim_c, *dft_ctx)
            yg, ygc = _gla(p, gup, gb, gla_norm_g[e].reshape(1, GLA_DV))
            xs = [_with_ctx_slab(yh, yhc), _with_ctx_slab(yg, ygc)]
        else:
            o = i // 2
            w_out = attn_w_out[o].astype(bf16)
            qn = attn_q_norm[o][_HEAD_PERM].reshape(1, LANE)
            kn = attn_k_norm[o][_HEAD_PERM].reshape(1, LANE)
            q, k, v = _attn_proj(h, mods, i, _attn_w_in_permuted(attn_w_in[o]), qn, kn, cos, sin)
            att = _attention(q, k, v, True)
            xs = [_with_ctx_slab(att, _attention(q, k, v, False)) if need_ctx else att]
        h = _mix_ffn(xs, w_out, h, mods, i, w1, w2, lng, lnb, NB if need_ctx else BATCH)
    return h
```

```python
import functools
import math

import jax
import jax.numpy as jnp
import numpy as np
from jax import lax
from jax.experimental import pallas as pl
from jax.experimental.pallas import tpu as pltpu

f32 = jnp.float32
bf16 = jnp.bfloat16

D_MODEL = 1024
BATCH = 8
SEQ = 2048
DEPTH = 4
GRID_W = 64
CTX_LEN = 256
N_MOD = 9
DEEPNORM_ALPHA = (2.0 * DEPTH) ** 0.25
LN_EPS = 1e-5
RMS_EPS = 1e-6
MACARON_WEIGHT = 0.5
FFN_HIDDEN = 2816

HYENA_WIDTH = 512
HYENA_PROJ = 3 * HYENA_WIDTH
HYENA_BANDS = 16
HYENA_EMB = 1 + 2 * HYENA_BANDS
HYENA_FILTER_HIDDEN = 64
HYENA_FAST_DECAY = 0.3
HYENA_SLOW_DECAY = 1.5
HYENA_TARGET = 1e-2

GLA_WIDTH = 512
GLA_HEADS = 4
GLA_DV = 128
GLA_DK = 64
GLA_QK = GLA_HEADS * GLA_DK
GLA_RANK = 16
GLA_TAU = 16.0
GLA_CHUNK = 64

ATTN_HEAD_DIM = 128
ATTN_Q_HEADS = 8
ATTN_KV_HEADS = 2
ATTN_GROUP = 4
ATTN_Q_WIDTH = 1024
ATTN_KV_WIDTH = 256
ROPE_THETA = 10000.0
ROPE_AXIS_PAIRS = 32

assert BATCH * CTX_LEN == SEQ
NB = BATCH + 1

LANE = 128
SUBLANE = 8
MOD_ROWS = 16
EMB_PAD = 128
VMEM_LIMIT = 56 * 1024 * 1024
TOKEN_TILE = 512

EVEN_Q = HYENA_PROJ
EVEN_K = EVEN_Q + GLA_QK
EVEN_V = EVEN_K + GLA_QK
EVEN_G = EVEN_V + GLA_WIDTH
EVEN_LO = EVEN_G + GLA_WIDTH
EVEN_W = EVEN_LO + LANE
HY_CBLK = 256
HY_FT = 512
V_EXT = 2 * LANE


def _cparams(*sem):
    return pltpu.CompilerParams(dimension_semantics=sem, vmem_limit_bytes=VMEM_LIMIT)


def _resident(shape, index=None):
    index = tuple(index or ())
    block = (None,) * len(index) + tuple(shape[len(index):])
    tail = (0,) * (len(shape) - len(index))
    return pl.BlockSpec(block, lambda *_: index + tail, pipeline_mode=pl.Buffered(1))


def _dot(a, b):
    return jnp.dot(a, b, preferred_element_type=f32)


def _dot_nt(a, b):
    return lax.dot_general(a, b, (((1,), (1,)), ((), ())), preferred_element_type=f32)


def _dot_tn(a, b):
    return lax.dot_general(a, b, (((0,), (0,)), ((), ())), preferred_element_type=f32)


def _split3(x):
    hi = x.astype(bf16)
    r1 = x - hi.astype(f32)
    mid = r1.astype(bf16)
    lo = (r1 - mid.astype(f32)).astype(bf16)
    return hi, mid, lo


def _dot_f32(a, b):
    a0, a1, a2 = _split3(a)
    b0, b1, b2 = _split3(b)
    return (_dot(a0, b0) + (_dot(a0, b1) + _dot(a1, b0))
            + (_dot(a0, b2) + _dot(a1, b1) + _dot(a2, b0)))


def _layer_norm(x, g, b):
    mu = jnp.mean(x, -1, keepdims=True)
    xc = x - mu
    var = jnp.mean(xc * xc, -1, keepdims=True)
    return xc * lax.rsqrt(var + LN_EPS) * g + b


def _mod_rows(mod_ref, slot):
    return (mod_ref[3 * slot:3 * slot + 1, :], mod_ref[3 * slot + 1:3 * slot + 2, :],
            mod_ref[3 * slot + 2:3 * slot + 3, :])


def _mod_spec(layer):
    return pl.BlockSpec((None, None, N_MOD, D_MODEL), lambda b, *_: (layer, b, 0, 0))


def _tok_spec(width):
    return pl.BlockSpec((None, TOKEN_TILE, width), lambda b, t: (b, t, 0))


def _ln_specs(layer, slot):
    return [_resident((DEPTH, 3, 1, D_MODEL), (layer, slot))] * 2


def _adaln_kernel(c_ref, w_ref, b_ref, o_ref):
    c = c_ref[...]
    sc = (c * jax.nn.sigmoid(c)).astype(bf16)
    o_ref[...] = _dot(sc, w_ref[...].astype(bf16)) + b_ref[...]


def _adaln(c_all, ada_w, ada_b):
    return pl.pallas_call(
        _adaln_kernel,
        grid=(DEPTH, N_MOD),
        in_specs=[
            pl.BlockSpec((MOD_ROWS, D_MODEL), lambda l, j: (0, 0)),
            pl.BlockSpec((None, D_MODEL, D_MODEL), lambda l, j: (l, 0, j)),
            pl.BlockSpec((None, 1, D_MODEL), lambda l, j: (l, 0, j)),
        ],
        out_specs=pl.BlockSpec((None, MOD_ROWS, D_MODEL), lambda l, j: (l, 0, j)),
        out_shape=jax.ShapeDtypeStruct((DEPTH, MOD_ROWS, N_MOD * D_MODEL), f32),
        compiler_params=_cparams("parallel", "parallel"),
        name="adaln",
    )(c_all, ada_w, ada_b.reshape(DEPTH, 1, N_MOD * D_MODEL))


FFN_TH = 256


def _ffn_sublayer(h, mod_ref, slot, w1_ref, w2_ref, g_ref, b_ref):
    shift, scale, gate = _mod_rows(mod_ref, slot)
    a = (h * (1.0 + scale) + shift).astype(bf16)
    acc = None
    for lo in range(0, FFN_HIDDEN, FFN_TH):
        gt = _dot(a, w1_ref[:, lo:lo + FFN_TH].astype(bf16))
        up = _dot(a, w1_ref[:, FFN_HIDDEN + lo:FFN_HIDDEN + lo + FFN_TH].astype(bf16))
        act = (gt * jax.nn.sigmoid(gt) * up).astype(bf16)
        part = _dot(act, w2_ref[lo:lo + FFN_TH, :].astype(bf16))
        acc = part if acc is None else acc + part
    return _layer_norm(DEEPNORM_ALPHA * h + gate * (MACARON_WEIGHT * acc), g_ref[...], b_ref[...])


def _ffn_kernel(slot, split_input, *refs):
    if split_input:
        x_ref, c_ref, mod_ref, w1_ref, w2_ref, g_ref, b_ref, o_ref = refs
        h = jnp.where(pl.program_id(0) == BATCH, c_ref[...], x_ref[...])
    else:
        h_ref, mod_ref, w1_ref, w2_ref, g_ref, b_ref, o_ref = refs
        h = h_ref[...]
    o_ref[...] = _ffn_sublayer(h, mod_ref, slot, w1_ref, w2_ref, g_ref, b_ref)


def _ffn(h, mods, layer, slot, which, w1, w2, ln_g, ln_b, n_slabs):
    split_input = isinstance(h, tuple)
    if split_input:
        tok_specs = [pl.BlockSpec((None, TOKEN_TILE, D_MODEL), lambda b, t: (jnp.minimum(b, BATCH - 1), t, 0)),
                     pl.BlockSpec((None, TOKEN_TILE, D_MODEL), lambda b, t: (0, t, 0))]
        hs = h
    else:
        tok_specs, hs = [_tok_spec(D_MODEL)], (h,)
    return pl.pallas_call(
        functools.partial(_ffn_kernel, slot, split_input),
        grid=(n_slabs, SEQ // TOKEN_TILE),
        in_specs=tok_specs + [_mod_spec(layer), _resident(w1.shape, (layer, which)),
                              _resident(w2.shape, (layer, which)), *_ln_specs(layer, slot)],
        out_specs=_tok_spec(D_MODEL),
        out_shape=jax.ShapeDtypeStruct((n_slabs, SEQ, D_MODEL), f32),
        compiler_params=_cparams("parallel", "parallel"),
        name="ffn",
    )(*hs, mods, w1, w2, ln_g, ln_b)


def _mix_ffn_kernel(n_in, *refs):
    x_refs, wo_refs = refs[:n_in], refs[n_in:2 * n_in]
    h_ref, mod_ref, g1_ref, b1_ref, w1_ref, w2_ref, g2_ref, b2_ref, o_ref = refs[2 * n_in:]
    _, _, gate = _mod_rows(mod_ref, 1)
    y = None
    for x_ref, wo_ref in zip(x_refs, wo_refs):
        part = _dot(x_ref[...], wo_ref[...])
        y = part if y is None else y + part
    h1 = _layer_norm(DEEPNORM_ALPHA * h_ref[...] + gate * y, g1_ref[...], b1_ref[...])
    o_ref[...] = _ffn_sublayer(h1, mod_ref, 2, w1_ref, w2_ref, g2_ref, b2_ref)


def _mix_ffn(xs, w_out, h, mods, layer, w1, w2, ln_g, ln_b, n_slabs):
    in_specs, row = [], 0
    for x in xs:
        in_specs.append(_tok_spec(x.shape[-1]))
    for x in xs:
        width = x.shape[-1]
        in_specs.append(pl.BlockSpec((width, D_MODEL), lambda b, t, r=row // width: (r, 0),
                                     pipeline_mode=pl.Buffered(1)))
        row += width
    return pl.pallas_call(
        functools.partial(_mix_ffn_kernel, len(xs)),
        grid=(n_slabs, SEQ // TOKEN_TILE),
        in_specs=in_specs + [_tok_spec(D_MODEL), _mod_spec(layer), *_ln_specs(layer, 1),
                             _resident(w1.shape, (layer, 1)), _resident(w2.shape, (layer, 1)), *_ln_specs(layer, 2)],
        out_specs=_tok_spec(D_MODEL),
        out_shape=jax.ShapeDtypeStruct((n_slabs, SEQ, D_MODEL), f32),
        compiler_params=_cparams("parallel", "parallel"),
        name="mix_ffn",
    )(*xs, *([w_out] * len(xs)), h, mods, ln_g, ln_b, w1, w2, ln_g, ln_b)


def _proj_kernel(tn, h_ref, mod_ref, w_ref, o_ref):
    shift, scale, _ = _mod_rows(mod_ref, 1)
    a = (h_ref[...] * (1.0 + scale) + shift).astype(bf16)
    n = w_ref.shape[1]
    for j in range(0, n, tn):
        w = min(tn, n - j)
        o_ref[:, j:j + w] = _dot(a, w_ref[:, j:j + w])


def _proj(h, mods, layer, w):
    n = w.shape[1]
    return pl.pallas_call(
        functools.partial(_proj_kernel, 512),
        grid=(NB, SEQ // TOKEN_TILE),
        in_specs=[_tok_spec(D_MODEL), _mod_spec(layer), _resident(w.shape)],
        out_specs=_tok_spec(n),
        out_shape=jax.ShapeDtypeStruct((NB, SEQ, n), f32),
        compiler_params=_cparams("parallel", "parallel"),
        name="even_proj",
    )(h, mods, w)


ATTN_PROJ_TILE = 256
Q_PRESCALE = ATTN_HEAD_DIM ** -0.5 * math.log2(math.e)


def _attn_proj_kernel(h_ref, mod_ref, w_ref, qn_ref, kn_ref, cos_ref, sin_ref, q_ref, k_ref, v_ref):
    shift, scale, _ = _mod_rows(mod_ref, 1)
    a = (h_ref[...] * (1.0 + scale) + shift).astype(bf16)
    is_ctx = pl.program_id(0) == BATCH
    cos = jnp.where(is_ctx, 1.0, cos_ref[...])
    sin = jnp.where(is_ctx, 0.0, sin_ref[...])

    def norm_rope(x, gain):
        x = x * lax.rsqrt(jnp.mean(x * x, -1, keepdims=True) + RMS_EPS) * gain
        return x * cos + pltpu.roll(x, ATTN_HEAD_DIM // 2, 1) * sin

    for half in range(2):
        x4 = _dot(a, w_ref[:, half * 512:(half + 1) * 512])
        for hh in range(4):
            hd = half * 4 + hh
            q_ref[:, hd * LANE:(hd + 1) * LANE] = (
                norm_rope(x4[:, hh * LANE:(hh + 1) * LANE], qn_ref[...]) * Q_PRESCALE).astype(bf16)
    kx = _dot(a, w_ref[:, ATTN_Q_WIDTH:ATTN_Q_WIDTH + ATTN_KV_WIDTH])
    for hd in range(ATTN_KV_HEADS):
        k_ref[:, hd * LANE:(hd + 1) * LANE] = norm_rope(kx[:, hd * LANE:(hd + 1) * LANE], kn_ref[...]).astype(bf16)
    vx = _dot(a, w_ref[:, ATTN_Q_WIDTH + ATTN_KV_WIDTH:ATTN_Q_WIDTH + 2 * ATTN_KV_WIDTH])
    ones = jnp.ones((a.shape[0], LANE), bf16)
    for hd in range(ATTN_KV_HEADS):
        v_ref[:, hd * V_EXT:hd * V_EXT + LANE] = vx[:, hd * LANE:(hd + 1) * LANE].astype(bf16)
        v_ref[:, hd * V_EXT + LANE:(hd + 1) * V_EXT] = ones


def _attn_proj(h, mods, layer, w, qn, kn, cos, sin):
    tm = ATTN_PROJ_TILE
    tok = lambda width: pl.BlockSpec((None, tm, width), lambda b, t: (b, t, 0))
    rope = pl.BlockSpec((tm, LANE), lambda b, t: (t, 0))
    return pl.pallas_call(
        _attn_proj_kernel,
        grid=(NB, SEQ // tm),
        in_specs=[tok(D_MODEL), _mod_spec(layer), _resident(w.shape), _resident((1, LANE)), _resident((1, LANE)),
                  rope, rope],
        out_specs=[tok(ATTN_Q_WIDTH), tok(ATTN_KV_WIDTH), tok(ATTN_KV_HEADS * V_EXT)],
        out_shape=[jax.ShapeDtypeStruct((NB, SEQ, ATTN_Q_WIDTH), bf16),
                   jax.ShapeDtypeStruct((NB, SEQ, ATTN_KV_WIDTH), bf16),
                   jax.ShapeDtypeStruct((NB, SEQ, ATTN_KV_HEADS * V_EXT), bf16)],
        compiler_params=_cparams("parallel", "parallel"),
        name="attn_proj",
    )(h, mods, w, qn, kn, cos, sin)


ATTN_TQ = 512


def _latent_slabs_only(slab, o_ref, body):
    @pl.when(slab < BATCH)
    def _():
        body()

    @pl.when(slab >= BATCH)
    def _():
        o_ref[...] = jnp.zeros_like(o_ref)


def _attn_kernel(has_lat, q_ref, kc_ref, vc_ref, *rest):
    if has_lat:
        k_ref, v_ref, o_ref = rest
        _latent_slabs_only(pl.program_id(0), o_ref,
                           functools.partial(_attn_heads, q_ref, kc_ref, vc_ref, k_ref, v_ref, o_ref))
    else:
        (o_ref,) = rest
        _attn_heads(q_ref, kc_ref, vc_ref, None, None, o_ref)


def _attn_heads(q_ref, kc_ref, vc_ref, k_ref, v_ref, o_ref):
    has_lat = k_ref is not None
    for g in range(ATTN_KV_HEADS):
        cols = slice(g * LANE, (g + 1) * LANE)
        vcols = slice(g * V_EXT, (g + 1) * V_EXT)
        for hh in range(ATTN_GROUP):
            hcols = slice((g * ATTN_GROUP + hh) * LANE, (g * ATTN_GROUP + hh + 1) * LANE)
            q = q_ref[:, hcols]
            sc = _dot_nt(q, kc_ref[:, cols])
            m = jnp.max(sc, -1, keepdims=True)
            if has_lat:
                s = _dot_nt(q, k_ref[:, cols])
                m = jnp.maximum(m, jnp.max(s, -1, keepdims=True))
                ov = _dot(jnp.exp2(s - m).astype(bf16), v_ref[:, vcols])
                ov = ov + _dot(jnp.exp2(sc - m).astype(bf16), vc_ref[:, vcols])
            else:
                ov = _dot(jnp.exp2(sc - m).astype(bf16), vc_ref[:, vcols])
            o_ref[:, hcols] = (ov[:, :LANE] / ov[:, LANE:]).astype(bf16)


def _attention(q, k, v, latent, n_slabs=BATCH):
    tq = ATTN_TQ if latent else CTX_LEN
    vw = ATTN_KV_HEADS * V_EXT
    if latent:
        grid = (n_slabs, SEQ // tq)
        lat = lambda b: jnp.minimum(b, BATCH - 1)
        qspec = pl.BlockSpec((None, tq, ATTN_Q_WIDTH), lambda b, t: (lat(b), t, 0))
        ctx_kv = lambda width: pl.BlockSpec((None, CTX_LEN, width), lambda b, t: (BATCH, lat(b), 0))
        lat_kv = lambda width: pl.BlockSpec((None, SEQ, width), lambda b, t: (lat(b), 0, 0))
        in_specs = [qspec, ctx_kv(ATTN_KV_WIDTH), ctx_kv(vw), lat_kv(ATTN_KV_WIDTH), lat_kv(vw)]
        args = (q, k, v, k, v)
        out_shape = jax.ShapeDtypeStruct((n_slabs, SEQ, ATTN_Q_WIDTH), bf16)
    else:
        grid = (BATCH, 1)
        qspec = pl.BlockSpec((None, tq, ATTN_Q_WIDTH), lambda b, t: (BATCH, b, 0))
        ctx_kv = lambda width: pl.BlockSpec((None, CTX_LEN, width), lambda b, t: (BATCH, b, 0))
        in_specs = [qspec, ctx_kv(ATTN_KV_WIDTH), ctx_kv(vw)]
        args = (q, k, v)
        out_shape = jax.ShapeDtypeStruct((BATCH, CTX_LEN, ATTN_Q_WIDTH), bf16)
    return pl.pallas_call(
        functools.partial(_attn_kernel, latent),
        grid=grid,
        in_specs=in_specs,
        out_specs=pl.BlockSpec((None, tq, ATTN_Q_WIDTH), lambda b, t: (b, t, 0)),
        out_shape=out_shape,
        compiler_params=_cparams("parallel", "parallel"),
        name="attention",
    )(*args)


def _with_ctx_slab(lat, ctx):
    return lat.at[BATCH].set(ctx.reshape(SEQ, ctx.shape[-1]))


def _dft_tables(length):
    idx = np.arange(length, dtype=np.int64)
    ang = (np.outer(idx, idx) % (2 * length)).astype(np.float64) * (math.pi / length)
    return tuple(jnp.asarray(t, dtype=f32).astype(bf16) for t in (np.cos(ang), -np.sin(ang)))


def _filter_kernel(length, feats_ref, t_ref, dl_ref, w1_ref, b1_ref, fr1_ref, w2_ref, b2_ref, fr2_ref,
                   w3_ref, b3_ref, cos_ref, sin_ref, kre_ref, kim_ref):
    h = jnp.sin(fr1_ref[...] * (_dot_f32(feats_ref[...], w1_ref[...]) + b1_ref[...]))
    h = jnp.sin(fr2_ref[...] * (_dot_f32(h, w2_ref[...]) + b2_ref[...]))
    h = _dot_f32(h, w3_ref[...]) + b3_ref[...]
    win = jnp.exp(-t_ref[...] * dl_ref[...])
    row = lax.broadcasted_iota(jnp.int32, (length, HYENA_WIDTH), 0)
    hf = h[:, :HYENA_WIDTH] * win
    hb = jnp.where(row == 0, 0.0, h[:, HYENA_WIDTH:] * win)
    inv = lax.rsqrt(jnp.sum(hf * hf + hb * hb, 0, keepdims=True) + 1e-6)
    even = (hf + hb) * inv
    odd = (hf - hb) * inv
    sign = jnp.where((row & 1) == 0, 1.0, -1.0)
    nyq = jnp.sum(even * sign, 0, keepdims=True)
    wgt = jnp.where(row == 0, 1.0, 2.0) * (0.5 / length)
    e0, e1, _ = _split3(even)
    o0, o1, _ = _split3(odd)
    cm, sm = cos_ref[...], sin_ref[...]
    kre = _dot(cm, e0) + _dot(cm, e1)
    kim = _dot(sm, o0) + _dot(sm, o1)
    kre_ref[...] = kre * wgt
    kim_ref[...] = jnp.where(row == 0, nyq, kim) * wgt


def _hyena_filter(length, w1, b1, fr1, w2, b2, fr2, w3, b3, cos_m, sin_m):
    n = np.arange(length, dtype=np.float32)[:, None]
    t = np.linspace(0.0, 1.0, length, dtype=np.float32)[:, None]
    bands = np.linspace(1e-4, HYENA_BANDS - 1, HYENA_BANDS, dtype=np.float32)[None, :]
    ang = (np.float32(2.0 * math.pi) * n * bands / np.float32(length)).astype(np.float32)
    feats = np.zeros((length, EMB_PAD), np.float32)
    feats[:, :HYENA_EMB] = np.concatenate([t, np.cos(ang), -np.sin(ang)], axis=-1)
    max_decay = math.log(HYENA_TARGET) / HYENA_FAST_DECAY
    min_decay = math.log(HYENA_TARGET) / HYENA_SLOW_DECAY
    deltas = np.abs(np.linspace(min_decay, max_decay, HYENA_WIDTH, dtype=np.float32))[None, :]
    w1p = jnp.zeros((EMB_PAD, HYENA_FILTER_HIDDEN), f32).at[:HYENA_EMB].set(w1)
    row = lambda v: v.reshape(1, -1)
    args = (jnp.asarray(feats), jnp.asarray(t), jnp.asarray(deltas), w1p, row(b1), row(fr1), w2, row(b2), row(fr2),
            w3, row(b3), cos_m, sin_m)
    out = jax.ShapeDtypeStruct((length, HYENA_WIDTH), f32)
    return pl.pallas_call(
        functools.partial(_filter_kernel, length),
        out_shape=[out, out],
        compiler_params=pltpu.CompilerParams(vmem_limit_bytes=VMEM_LIMIT),
        name="hyena_filter",
    )(*args)


def _conv3(p_ref, w_ref, b_ref, length):
    x = p_ref[...]
    row = lax.broadcasted_iota(jnp.int32, x.shape, 0)
    prev = jnp.where(row == 0, 0.0, pltpu.roll(x, 1, 0))
    nxt = jnp.where(row == length - 1, 0.0, pltpu.roll(x, length - 1, 0))
    return prev * w_ref[0:1, :] + x * w_ref[1:2, :] + nxt * w_ref[2:3, :] + b_ref[...]


def _hyena_kernel(length, *refs):
    if length == SEQ:
        _latent_slabs_only(pl.program_id(1), refs[-3], functools.partial(_hyena_body, length, *refs))
    else:
        _hyena_body(length, *refs)


def _hyena_body(length, p0_ref, p1_ref, p2_ref, w0_ref, w1_ref, w2_ref, b0_ref, b1_ref, b2_ref, skip_ref,
                kre_ref, kim_ref, cos_ref, sin_ref, o_ref, yre_ref, yim_ref):
    z = _conv3(p0_ref, w0_ref, b0_ref, length) * _conv3(p1_ref, w1_ref, b1_ref, length)
    zb = z.astype(bf16)
    ft = min(length, HY_FT)
    for f0 in range(0, length, ft):
        rows = slice(f0, f0 + ft)
        zre = _dot(cos_ref[rows, :], zb)
        zim = _dot(sin_ref[rows, :], zb)
        kre, kim = kre_ref[rows, :], kim_ref[rows, :]
        yre_ref[rows, :] = (zre * kre - zim * kim).astype(bf16)
        yim_ref[rows, :] = (zre * kim + zim * kre).astype(bf16)
    row = lax.broadcasted_iota(jnp.int32, z.shape, 0)
    sign = jnp.where((row & 1) == 0, 1.0, -1.0)
    y_nyq = jnp.sum(z * sign, 0, keepdims=True) * kim_ref[0:1, :]
    y = _dot(cos_ref[...], yre_ref[...]) + _dot(sin_ref[...], yim_ref[...]) + sign * y_nyq
    o_ref[...] = ((y + z * skip_ref[...]) * _conv3(p2_ref, w2_ref, b2_ref, length)).astype(bf16)


def _hyena(p, length, conv_w, conv_b, skip, kre, kim, cos_m, sin_m):
    nblk = HYENA_WIDTH // HY_CBLK
    n_out = NB if length == SEQ else BATCH
    if length == SEQ:
        part = lambda j: pl.BlockSpec((None, length, HY_CBLK),
                                      lambda c, b: (jnp.minimum(b, BATCH - 1), 0, j * nblk + c))
    else:
        part = lambda j: pl.BlockSpec((None, length, HY_CBLK), lambda c, b: (BATCH, b, j * nblk + c))
    wpart = lambda j, rows: pl.BlockSpec((rows, HY_CBLK), lambda c, b: (0, j * nblk + c))
    chan = pl.BlockSpec((1, HY_CBLK), lambda c, b: (0, c))
    spec_k = pl.BlockSpec((length, HY_CBLK), lambda c, b: (0, c), pipeline_mode=pl.Buffered(1))
    table = _resident((length, length))
    return pl.pallas_call(
        functools.partial(_hyena_kernel, length),
        grid=(nblk, n_out),
        in_specs=[part(0), part(1), part(2), wpart(0, 3), wpart(1, 3), wpart(2, 3), wpart(0, 1), wpart(1, 1),
                  wpart(2, 1), chan, spec_k, spec_k, table, table],
        out_specs=pl.BlockSpec((None, length, HY_CBLK), lambda c, b: (b, 0, c)),
        out_shape=jax.ShapeDtypeStruct((n_out, length, HYENA_WIDTH), bf16),
        scratch_shapes=[pltpu.VMEM((length, HY_CBLK), bf16), pltpu.VMEM((length, HY_CBLK), bf16)],
        compiler_params=_cparams("parallel", "parallel"),
        name="hyena",
    )(p, p, p, conv_w, conv_w, conv_w, conv_b, conv_b, conv_b, skip, kre, kim, cos_m, sin_m)


GLA_GROUP_LAT = 4
GLA_GROUP_CTX = 2
GLA_HEADS_PER_TILE = LANE // GLA_DK
GLA_PRE_TILE = 256


def _log_sigmoid(x):
    return jnp.minimum(x, 0.0) - jnp.log(1.0 + jnp.exp(-jnp.abs(x)))


def _head_tile(hd):
    tile = hd // GLA_HEADS_PER_TILE
    lane = lax.broadcasted_iota(jnp.int32, (1, LANE), 1)
    first = (hd % GLA_HEADS_PER_TILE) * GLA_DK
    return slice(tile * LANE, (tile + 1) * LANE), (lane >= first) & (lane < first + GLA_DK)


def _aligned(x, m):
    return x if isinstance(x, int) else pl.multiple_of(x, m)


def _chunk_tri(n_rows, direction):
    ri = lax.broadcasted_iota(jnp.int32, (n_rows, n_rows), 0)
    ci = lax.broadcasted_iota(jnp.int32, (n_rows, n_rows), 1)
    same_chunk = (ri // GLA_CHUNK) == (ci // GLA_CHUNK)
    return same_chunk & ((ci <= ri) if direction == 0 else (ci >= ri))


def _gla_decay_tile(direction, refs, gup_ref, gb_ref, scr, rb, sb):
    q_ref, k_ref, _, _, lo_ref = refs
    qin_ref, kin_ref, kst_ref, dec_ref = scr
    group = GLA_PRE_TILE // GLA_CHUNK
    rows = pl.ds(rb, GLA_PRE_TILE)
    tri_b = jnp.where(_chunk_tri(GLA_PRE_TILE, direction), 1.0, 0.0).astype(bf16)
    x = _dot(lo_ref[rows, :].astype(bf16), gup_ref[direction]) + gb_ref[direction]
    lg = _log_sigmoid(x) * (1.0 / GLA_TAU)
    l0, l1, _ = _split3(lg)
    cum = _dot(tri_b, l0) + _dot(tri_b, l1)
    tots = []
    for g in range(group):
        end = g * GLA_CHUNK + (GLA_CHUNK - 1 if direction == 0 else 0)
        tots.append(cum[end:end + 1, :])
    tot_rows = jnp.concatenate([jnp.broadcast_to(t, (GLA_CHUNK, t.shape[1])) for t in tots], 0)
    k = k_ref[rows, :]
    srows = pl.ds(sb, GLA_PRE_TILE)
    qin_ref[direction, srows, :] = (q_ref[rows, :] * (GLA_DK ** -0.5) * jnp.exp(cum)).astype(bf16)
    kin_ref[direction, srows, :] = (k * jnp.exp(-cum)).astype(bf16)
    kst_ref[direction, srows, :] = (k * jnp.exp(tot_rows - cum)).astype(bf16)
    for g in range(group):
        drow = _aligned((sb // GLA_CHUNK + g) * SUBLANE, SUBLANE)
        dec_ref[direction, pl.ds(drow, SUBLANE), :] = jnp.broadcast_to(jnp.exp(tots[g]), (SUBLANE, GLA_QK))


def _gla_block_prep(direction, refs, scr, rb, sb, group):
    v_ref = refs[2]
    qin_ref, kin_ref, kst_ref, dec_ref = scr
    n_rows = group * GLA_CHUNK
    tri = _chunk_tri(n_rows, direction)
    srows = pl.ds(sb, n_rows)
    q_in, k_in, k_st = qin_ref[direction, srows, :], kin_ref[direction, srows, :], kst_ref[direction, srows, :]
    dec = dec_ref[direction, pl.ds(_aligned(sb // GLA_CHUNK * SUBLANE, group * SUBLANE), group * SUBLANE), :]
    decs = [dec[g * SUBLANE:g * SUBLANE + 1, :] for g in range(group)]
    v = v_ref[pl.ds(rb, n_rows), :].astype(bf16)
    q_heads, intra, ds_t = [], [], [[] for _ in range(group)]
    for hd in range(GLA_HEADS):
        cols, lanes = _head_tile(hd)
        vc = slice(hd * GLA_DV, (hd + 1) * GLA_DV)
        q_h = jnp.where(lanes, q_in[:, cols], 0.0).astype(bf16)
        kst_h = jnp.where(lanes, k_st[:, cols], 0.0).astype(bf16)
        att = jnp.where(tri, _dot_nt(q_h, k_in[:, cols]), 0.0)
        q_heads.append(q_h)
        intra.append(_dot(att.astype(bf16), v[:, vc]))
        for g in range(group):
            cr = slice(g * GLA_CHUNK, (g + 1) * GLA_CHUNK)
            ds_t[g].append(_dot_tn(v[cr, vc], kst_h[cr, :]))
    return q_heads, intra, ds_t, decs


def _gla_block_scan(direction, prep, st_ref, group):
    q_heads, intra, ds_t, decs = prep
    order = range(group) if direction == 0 else range(group - 1, -1, -1)
    outs = []
    for hd in range(GLA_HEADS):
        cols, _ = _head_tile(hd)
        st = st_ref[direction, hd]
        inter = [None] * group
        for g in order:
            cr = slice(g * GLA_CHUNK, (g + 1) * GLA_CHUNK)
            inter[g] = _dot_nt(q_heads[hd][cr, :], st.astype(bf16))
            st = st * decs[g][:, cols] + ds_t[g][hd]
        st_ref[direction, hd] = st
        outs.append(intra[hd] + jnp.concatenate(inter, 0))
    return jnp.concatenate(outs, axis=-1)


def _gla_finish(o, g, ng):
    outs = []
    for hd in range(GLA_HEADS):
        oh = o[:, hd * GLA_DV:(hd + 1) * GLA_DV]
        outs.append(oh * lax.rsqrt(jnp.mean(oh * oh, -1, keepdims=True) + RMS_EPS) * ng)
    return jnp.concatenate(outs, axis=-1) * (g * jax.nn.sigmoid(g))


def _gla_scan(refs, o_ref, acc_ref, scr, ng_ref, st_ref, n_chunks, group, s0):
    g_ref = refs[3]
    n_rows = group * GLA_CHUNK
    n_blocks = n_chunks // group
    half = n_blocks // 2

    def step(i, finish):
        bases = [pl.multiple_of(blk * n_rows, n_rows) for blk in (i, n_blocks - 1 - i)]
        preps = [_gla_block_prep(d, refs, scr, bases[d], _aligned(s0 + bases[d], n_rows), group) for d in (0, 1)]
        for d in (0, 1):
            rows = pl.ds(bases[d], n_rows)
            o = _gla_block_scan(d, preps[d], st_ref, group)
            if finish:
                o_ref[rows, :] = _gla_finish(acc_ref[rows, :] + o, g_ref[rows, :], ng_ref[...]).astype(bf16)
            else:
                acc_ref[rows, :] = o

    def first(i, carry):
        step(i, False)
        return carry

    def second(i, carry):
        step(i, True)
        return carry

    lax.fori_loop(0, half, first, 0)
    lax.fori_loop(half, n_blocks, second, 0)


def _gla_kernel(*refs):
    _latent_slabs_only(pl.program_id(0), refs[13], functools.partial(_gla_sequence, *refs))


def _gla_sequence(*refs):
    lat, ctx = refs[0:5], refs[5:10]
    gup_ref, gb_ref, ng_ref, o_ref, oc_ref, st_ref, acc_ref = refs[10:17]
    scr = refs[17:]
    st_ref[...] = jnp.zeros_like(st_ref)

    def decay_tiles(t, carry):
        rb = pl.multiple_of(t * GLA_PRE_TILE, GLA_PRE_TILE)
        for d in (0, 1):
            _gla_decay_tile(d, lat, gup_ref, gb_ref, scr, rb, rb)
        return carry

    for d in (0, 1):
        _gla_decay_tile(d, ctx, gup_ref, gb_ref, scr, 0, SEQ)
    lax.fori_loop(0, SEQ // GLA_PRE_TILE, decay_tiles, 0)
    _gla_scan(ctx, oc_ref, acc_ref, scr, ng_ref, st_ref, CTX_LEN // GLA_CHUNK, GLA_GROUP_CTX, SEQ)
    _gla_scan(lat, o_ref, acc_ref, scr, ng_ref, st_ref, SEQ // GLA_CHUNK, GLA_GROUP_LAT, 0)


def _gla(p, gup, gb, ng):
    seq = lambda b: jnp.minimum(b, BATCH - 1)

    def pieces(latent):
        ln = SEQ if latent else CTX_LEN
        if latent:
            blk = lambda width, off: pl.BlockSpec((None, ln, width), lambda b: (seq(b), 0, off // width))
        else:
            blk = lambda width, off: pl.BlockSpec((None, ln, width), lambda b: (BATCH, seq(b), off // width))
        return [blk(GLA_QK, EVEN_Q), blk(GLA_QK, EVEN_K), blk(GLA_WIDTH, EVEN_V), blk(GLA_WIDTH, EVEN_G),
                blk(LANE, EVEN_LO)]

    all_rows = SEQ + CTX_LEN
    return pl.pallas_call(
        _gla_kernel,
        grid=(NB,),
        in_specs=pieces(True) + pieces(False) + [_resident(gup.shape), _resident(gb.shape), _resident(ng.shape)],
        out_specs=[pl.BlockSpec((None, SEQ, GLA_WIDTH), lambda b: (b, 0, 0)),
                   pl.BlockSpec((None, CTX_LEN, GLA_WIDTH), lambda b: (seq(b), 0, 0))],
        out_shape=[jax.ShapeDtypeStruct((NB, SEQ, GLA_WIDTH), bf16),
                   jax.ShapeDtypeStruct((BATCH, CTX_LEN, GLA_WIDTH), bf16)],
        scratch_shapes=[pltpu.VMEM((2, GLA_HEADS, GLA_DV, LANE), f32), pltpu.VMEM((SEQ, GLA_WIDTH), f32),
                        pltpu.VMEM((2, all_rows, GLA_QK), bf16), pltpu.VMEM((2, all_rows, GLA_QK), bf16),
                        pltpu.VMEM((2, all_rows, GLA_QK), bf16),
                        pltpu.VMEM((2, all_rows // GLA_CHUNK * SUBLANE, GLA_QK), f32)],
        compiler_params=_cparams("arbitrary"),
        name="gla",
    )(*([p] * 10), gup, gb, ng)


def _even_w_in_aligned(w_in):
    main, lo = w_in[:, :EVEN_LO], w_in[:, EVEN_LO:]
    return jnp.concatenate([main, jnp.pad(lo, ((0, 0), (0, LANE - 2 * GLA_RANK)))], -1).astype(bf16)


def _gla_gate_params(gate_up, gate_b):
    ups = [jnp.zeros((LANE, GLA_QK), f32).at[j * GLA_RANK:(j + 1) * GLA_RANK].set(gate_up[j]) for j in range(2)]
    return jnp.stack(ups).astype(bf16), gate_b.reshape(2, 1, GLA_QK)


_HEAD_PERM = np.concatenate([np.arange(0, 32), np.arange(64, 96), np.arange(32, 64), np.arange(96, 128)])


def _attn_w_in_permuted(w_in):
    nqk = ATTN_Q_HEADS + ATTN_KV_HEADS
    cols = (np.arange(nqk)[:, None] * ATTN_HEAD_DIM + _HEAD_PERM[None, :]).reshape(-1)
    cols = np.concatenate([cols, np.arange(nqk * ATTN_HEAD_DIM, nqk * ATTN_HEAD_DIM + ATTN_KV_WIDTH)])
    return w_in[:, cols].astype(bf16)


def _rope_tables():
    rows = SEQ // GRID_W
    row = jnp.repeat(jnp.arange(rows, dtype=f32), GRID_W)
    col = jnp.tile(jnp.arange(GRID_W, dtype=f32), rows)
    inv_freq = ROPE_THETA ** (-jnp.arange(ROPE_AXIS_PAIRS, dtype=f32) / ROPE_AXIS_PAIRS)
    ang_r = row[:, None] * inv_freq
    ang_c = col[:, None] * inv_freq
    cos = jnp.concatenate([jnp.cos(ang_r), jnp.cos(ang_c), jnp.cos(ang_r), jnp.cos(ang_c)], -1)
    sin = jnp.concatenate([-jnp.sin(ang_r), -jnp.sin(ang_c), jnp.sin(ang_r), jnp.sin(ang_c)], -1)
    return cos, sin


def kernel(x, c, ctx, c_ctx, ada_w, ada_b, ln_g, ln_b, ffn_w1, ffn_w2, even_w_in, even_w_out, hyena_conv_w, hyena_conv_b, hyena_f_w1, hyena_f_b1, hyena_f_fr1, hyena_f_w2, hyena_f_b2, hyena_f_fr2, hyena_f_w3, hyena_f_b3, hyena_skip, gla_gate_up, gla_gate_b, gla_norm_g, attn_w_in, attn_w_out, attn_q_norm, attn_k_norm):
    c_all = jnp.concatenate([c, c_ctx[None], jnp.zeros((MOD_ROWS - NB, D_MODEL), f32)], 0)
    mods = _adaln(c_all, ada_w, ada_b).reshape(DEPTH, MOD_ROWS, N_MOD, D_MODEL)
    cos, sin = _rope_tables()
    dft_lat = _dft_tables(SEQ)
    dft_ctx = _dft_tables(CTX_LEN)
    w1, w2 = ffn_w1, ffn_w2
    lng = ln_g.reshape(DEPTH, 3, 1, D_MODEL)
    lnb = ln_b.reshape(DEPTH, 3, 1, D_MODEL)

    h = (x, ctx.reshape(1, SEQ, D_MODEL))
    for i in range(DEPTH):
        need_ctx = i < DEPTH - 1
        h = _ffn(h, mods, i, 0, 0, w1, w2, lng, lnb, NB)
        if i % 2 == 0:
            e = i // 2
            w_out = even_w_out[e].astype(bf16)
            gup, gb = _gla_gate_params(gla_gate_up[e], gla_gate_b[e])
            p = _proj(h, mods, i, _even_w_in_aligned(even_w_in[e]))
            filt = (hyena_f_w1[e], hyena_f_b1[e], hyena_f_fr1[e], hyena_f_w2[e], hyena_f_b2[e], hyena_f_fr2[e],
                    hyena_f_w3[e], hyena_f_b3[e])
            conv_b = hyena_conv_b[e].reshape(1, HYENA_PROJ)
            skip = hyena_skip[e].reshape(1, HYENA_WIDTH)
            kre, kim = _hyena_filter(SEQ, *filt, dft_lat[0], dft_lat[1])
            yh = _hyena(p, SEQ, hyena_conv_w[e], conv_b, skip, kre, kim, *dft_lat)
            kre_c, kim_c = _hyena_filter(CTX_LEN, *filt, dft_ctx[0], dft_ctx[1])
            yhc = _hyena(p, CTX_LEN, hyena_conv_w[e], conv_b, skip, kre_c, kim_c, *dft_ctx)
            yg, ygc = _gla(p, gup, gb, gla_norm_g[e].reshape(1, GLA_DV))
            xs = [_with_ctx_slab(yh, yhc), _with_ctx_slab(yg, ygc)]
        else:
            o = i // 2
            w_out = attn_w_out[o].astype(bf16)
            qn = attn_q_norm[o][_HEAD_PERM].reshape(1, LANE)
            kn = attn_k_norm[o][_HEAD_PERM].reshape(1, LANE)
            q, k, v = _attn_proj(h, mods, i, _attn_w_in_permuted(attn_w_in[o]), qn, kn, cos, sin)
            att = _attention(q, k, v, True, NB if need_ctx else BATCH)
            xs = [_with_ctx_slab(att, _attention(q, k, v, False)) if need_ctx else att]
        h = _mix_ffn(xs, w_out, h, mods, i, w1, w2, lng, lnb, NB if need_ctx else BATCH)
    return h
```

```python
import functools
import math

import jax
import jax.numpy as jnp
import numpy as np
from jax import lax
from jax.experimental import pallas as pl
from jax.experimental.pallas import tpu as pltpu

f32 = jnp.float32
bf16 = jnp.bfloat16

D_MODEL = 1024
BATCH = 8
SEQ = 2048
DEPTH = 4
GRID_W = 64
CTX_LEN = 256
N_MOD = 9
DEEPNORM_ALPHA = (2.0 * DEPTH) ** 0.25
LN_EPS = 1e-5
RMS_EPS = 1e-6
MACARON_WEIGHT = 0.5
FFN_HIDDEN = 2816

HYENA_WIDTH = 512
HYENA_PROJ = 3 * HYENA_WIDTH
HYENA_BANDS = 16
HYENA_EMB = 1 + 2 * HYENA_BANDS
HYENA_FILTER_HIDDEN = 64
HYENA_FAST_DECAY = 0.3
HYENA_SLOW_DECAY = 1.5
HYENA_TARGET = 1e-2

GLA_WIDTH = 512
GLA_HEADS = 4
GLA_DV = 128
GLA_DK = 64
GLA_QK = GLA_HEADS * GLA_DK
GLA_RANK = 16
GLA_TAU = 16.0
GLA_CHUNK = 64

ATTN_HEAD_DIM = 128
ATTN_Q_HEADS = 8
ATTN_KV_HEADS = 2
ATTN_GROUP = 4
ATTN_Q_WIDTH = 1024
ATTN_KV_WIDTH = 256
ROPE_THETA = 10000.0
ROPE_AXIS_PAIRS = 32

assert BATCH * CTX_LEN == SEQ
NB = BATCH + 1

LANE = 128
SUBLANE = 8
MOD_ROWS = 16
EMB_PAD = 128
VMEM_LIMIT = 56 * 1024 * 1024
TOKEN_TILE = 512
PROJ_TILE = 1024

EVEN_Q = HYENA_PROJ
EVEN_K = EVEN_Q + GLA_QK
EVEN_V = EVEN_K + GLA_QK
EVEN_G = EVEN_V + GLA_WIDTH
EVEN_LO = EVEN_G + GLA_WIDTH
EVEN_W = EVEN_LO + LANE
HY_CBLK = 256
HY_FT = 512
V_EXT = 2 * LANE


def _cparams(*sem):
    return pltpu.CompilerParams(dimension_semantics=sem, vmem_limit_bytes=VMEM_LIMIT)


def _resident(shape, index=None):
    index = tuple(index or ())
    block = (None,) * len(index) + tuple(shape[len(index):])
    tail = (0,) * (len(shape) - len(index))
    return pl.BlockSpec(block, lambda *_: index + tail, pipeline_mode=pl.Buffered(1))


def _dot(a, b):
    return jnp.dot(a, b, preferred_element_type=f32)


def _dot_nt(a, b):
    return lax.dot_general(a, b, (((1,), (1,)), ((), ())), preferred_element_type=f32)


def _dot_tn(a, b):
    return lax.dot_general(a, b, (((0,), (0,)), ((), ())), preferred_element_type=f32)


def _split3(x):
    hi = x.astype(bf16)
    r1 = x - hi.astype(f32)
    mid = r1.astype(bf16)
    lo = (r1 - mid.astype(f32)).astype(bf16)
    return hi, mid, lo


def _dot_f32(a, b):
    a0, a1, a2 = _split3(a)
    b0, b1, b2 = _split3(b)
    return (_dot(a0, b0) + (_dot(a0, b1) + _dot(a1, b0))
            + (_dot(a0, b2) + _dot(a1, b1) + _dot(a2, b0)))


def _layer_norm(x, g, b):
    mu = jnp.mean(x, -1, keepdims=True)
    xc = x - mu
    var = jnp.mean(xc * xc, -1, keepdims=True)
    return xc * lax.rsqrt(var + LN_EPS) * g + b


def _mod_rows(mod_ref, slot):
    return (mod_ref[3 * slot:3 * slot + 1, :], mod_ref[3 * slot + 1:3 * slot + 2, :],
            mod_ref[3 * slot + 2:3 * slot + 3, :])


def _mod_spec(layer):
    return pl.BlockSpec((None, None, N_MOD, D_MODEL), lambda b, *_: (layer, b, 0, 0))


def _tok_spec(width):
    return pl.BlockSpec((None, TOKEN_TILE, width), lambda b, t: (b, t, 0))


def _ln_specs(layer, slot):
    return [_resident((DEPTH, 3, 1, D_MODEL), (layer, slot))] * 2


def _adaln_kernel(c_ref, w_ref, b_ref, o_ref):
    c = c_ref[...]
    sc = (c * jax.nn.sigmoid(c)).astype(bf16)
    o_ref[...] = _dot(sc, w_ref[...].astype(bf16)) + b_ref[...]


def _adaln(c_all, ada_w, ada_b):
    return pl.pallas_call(
        _adaln_kernel,
        grid=(DEPTH, N_MOD),
        in_specs=[
            pl.BlockSpec((MOD_ROWS, D_MODEL), lambda l, j: (0, 0)),
            pl.BlockSpec((None, D_MODEL, D_MODEL), lambda l, j: (l, 0, j)),
            pl.BlockSpec((None, 1, D_MODEL), lambda l, j: (l, 0, j)),
        ],
        out_specs=pl.BlockSpec((None, MOD_ROWS, D_MODEL), lambda l, j: (l, 0, j)),
        out_shape=jax.ShapeDtypeStruct((DEPTH, MOD_ROWS, N_MOD * D_MODEL), f32),
        compiler_params=_cparams("parallel", "parallel"),
        name="adaln",
    )(c_all, ada_w, ada_b.reshape(DEPTH, 1, N_MOD * D_MODEL))


FFN_TH = 256


def _ffn_sublayer(h, mod_ref, slot, w1_ref, w2_ref, g_ref, b_ref):
    shift, scale, gate = _mod_rows(mod_ref, slot)
    a = (h * (1.0 + scale) + shift).astype(bf16)
    acc = None
    for lo in range(0, FFN_HIDDEN, FFN_TH):
        gt = _dot(a, w1_ref[:, lo:lo + FFN_TH].astype(bf16))
        up = _dot(a, w1_ref[:, FFN_HIDDEN + lo:FFN_HIDDEN + lo + FFN_TH].astype(bf16))
        act = (gt * jax.nn.sigmoid(gt) * up).astype(bf16)
        part = _dot(act, w2_ref[lo:lo + FFN_TH, :].astype(bf16))
        acc = part if acc is None else acc + part
    return _layer_norm(DEEPNORM_ALPHA * h + gate * (MACARON_WEIGHT * acc), g_ref[...], b_ref[...])


def _ffn_kernel(slot, split_input, *refs):
    if split_input:
        x_ref, c_ref, mod_ref, w1_ref, w2_ref, g_ref, b_ref, o_ref = refs
        h = jnp.where(pl.program_id(0) == BATCH, c_ref[...], x_ref[...])
    else:
        h_ref, mod_ref, w1_ref, w2_ref, g_ref, b_ref, o_ref = refs
        h = h_ref[...]
    o_ref[...] = _ffn_sublayer(h, mod_ref, slot, w1_ref, w2_ref, g_ref, b_ref)


def _ffn(h, mods, layer, slot, which, w1, w2, ln_g, ln_b, n_slabs):
    split_input = isinstance(h, tuple)
    if split_input:
        tok_specs = [pl.BlockSpec((None, TOKEN_TILE, D_MODEL), lambda b, t: (jnp.minimum(b, BATCH - 1), t, 0)),
                     pl.BlockSpec((None, TOKEN_TILE, D_MODEL), lambda b, t: (0, t, 0))]
        hs = h
    else:
        tok_specs, hs = [_tok_spec(D_MODEL)], (h,)
    return pl.pallas_call(
        functools.partial(_ffn_kernel, slot, split_input),
        grid=(n_slabs, SEQ // TOKEN_TILE),
        in_specs=tok_specs + [_mod_spec(layer), _resident(w1.shape, (layer, which)),
                              _resident(w2.shape, (layer, which)), *_ln_specs(layer, slot)],
        out_specs=_tok_spec(D_MODEL),
        out_shape=jax.ShapeDtypeStruct((n_slabs, SEQ, D_MODEL), f32),
        compiler_params=_cparams("parallel", "parallel"),
        name="ffn",
    )(*hs, mods, w1, w2, ln_g, ln_b)


def _mix_ffn_kernel(n_in, *refs):
    x_refs, wo_refs = refs[:n_in], refs[n_in:2 * n_in]
    h_ref, mod_ref, g1_ref, b1_ref, w1_ref, w2_ref, g2_ref, b2_ref, o_ref = refs[2 * n_in:]
    _, _, gate = _mod_rows(mod_ref, 1)
    y = None
    for x_ref, wo_ref in zip(x_refs, wo_refs):
        part = _dot(x_ref[...], wo_ref[...].astype(bf16))
        y = part if y is None else y + part
    h1 = _layer_norm(DEEPNORM_ALPHA * h_ref[...] + gate * y, g1_ref[...], b1_ref[...])
    o_ref[...] = _ffn_sublayer(h1, mod_ref, 2, w1_ref, w2_ref, g2_ref, b2_ref)


def _mix_ffn(xs, w_out, w_idx, h, mods, layer, w1, w2, ln_g, ln_b, n_slabs):
    in_specs, row = [], 0
    for x in xs:
        in_specs.append(_tok_spec(x.shape[-1]))
    for x in xs:
        width = x.shape[-1]
        in_specs.append(pl.BlockSpec((None, width, D_MODEL), lambda b, t, r=row // width: (w_idx, r, 0),
                                     pipeline_mode=pl.Buffered(1)))
        row += width
    return pl.pallas_call(
        functools.partial(_mix_ffn_kernel, len(xs)),
        grid=(n_slabs, SEQ // TOKEN_TILE),
        in_specs=in_specs + [_tok_spec(D_MODEL), _mod_spec(layer), *_ln_specs(layer, 1),
                             _resident(w1.shape, (layer, 1)), _resident(w2.shape, (layer, 1)), *_ln_specs(layer, 2)],
        out_specs=_tok_spec(D_MODEL),
        out_shape=jax.ShapeDtypeStruct((n_slabs, SEQ, D_MODEL), f32),
        compiler_params=_cparams("parallel", "parallel"),
        name="mix_ffn",
    )(*xs, *([w_out] * len(xs)), h, mods, ln_g, ln_b, w1, w2, ln_g, ln_b)


def _proj_kernel(tn, h_ref, mod_ref, w_ref, o_ref):
    shift, scale, _ = _mod_rows(mod_ref, 1)
    a = (h_ref[...] * (1.0 + scale) + shift).astype(bf16)
    n = w_ref.shape[1]
    for j in range(0, n, tn):
        w = min(tn, n - j)
        o_ref[:, j:j + w] = _dot(a, w_ref[:, j:j + w])


def _proj(h, mods, layer, w):
    n = w.shape[1]
    tok = lambda width: pl.BlockSpec((None, PROJ_TILE, width), lambda b, t: (b, t, 0))
    return pl.pallas_call(
        functools.partial(_proj_kernel, 512),
        grid=(NB, SEQ // PROJ_TILE),
        in_specs=[tok(D_MODEL), _mod_spec(layer), _resident(w.shape)],
        out_specs=tok(n),
        out_shape=jax.ShapeDtypeStruct((NB, SEQ, n), f32),
        compiler_params=_cparams("parallel", "parallel"),
        name="even_proj",
    )(h, mods, w)


ATTN_PROJ_TILE = 256
Q_PRESCALE = ATTN_HEAD_DIM ** -0.5 * math.log2(math.e)


def _attn_proj_kernel(h_ref, mod_ref, w_ref, qn_ref, kn_ref, cos_ref, sin_ref, q_ref, k_ref, v_ref):
    shift, scale, _ = _mod_rows(mod_ref, 1)
    a = (h_ref[...] * (1.0 + scale) + shift).astype(bf16)
    is_ctx = pl.program_id(0) == BATCH
    cos = jnp.where(is_ctx, 1.0, cos_ref[...])
    sin = jnp.where(is_ctx, 0.0, sin_ref[...])

    def norm_rope(x, gain):
        x = x * lax.rsqrt(jnp.mean(x * x, -1, keepdims=True) + RMS_EPS) * gain
        return x * cos + pltpu.roll(x, ATTN_HEAD_DIM // 2, 1) * sin

    for half in range(2):
        x4 = _dot(a, w_ref[:, half * 512:(half + 1) * 512])
        for hh in range(4):
            hd = half * 4 + hh
            q_ref[:, hd * LANE:(hd + 1) * LANE] = (
                norm_rope(x4[:, hh * LANE:(hh + 1) * LANE], qn_ref[...]) * Q_PRESCALE).astype(bf16)
    kx = _dot(a, w_ref[:, ATTN_Q_WIDTH:ATTN_Q_WIDTH + ATTN_KV_WIDTH])
    for hd in range(ATTN_KV_HEADS):
        k_ref[:, hd * LANE:(hd + 1) * LANE] = norm_rope(kx[:, hd * LANE:(hd + 1) * LANE], kn_ref[...]).astype(bf16)
    vx = _dot(a, w_ref[:, ATTN_Q_WIDTH + ATTN_KV_WIDTH:ATTN_Q_WIDTH + 2 * ATTN_KV_WIDTH])
    ones = jnp.ones((a.shape[0], LANE), bf16)
    for hd in range(ATTN_KV_HEADS):
        v_ref[:, hd * V_EXT:hd * V_EXT + LANE] = vx[:, hd * LANE:(hd + 1) * LANE].astype(bf16)
        v_ref[:, hd * V_EXT + LANE:(hd + 1) * V_EXT] = ones


def _attn_proj(h, mods, layer, w, qn, kn, cos, sin):
    tm = ATTN_PROJ_TILE
    tok = lambda width: pl.BlockSpec((None, tm, width), lambda b, t: (b, t, 0))
    rope = pl.BlockSpec((tm, LANE), lambda b, t: (t, 0))
    return pl.pallas_call(
        _attn_proj_kernel,
        grid=(NB, SEQ // tm),
        in_specs=[tok(D_MODEL), _mod_spec(layer), _resident(w.shape), _resident((1, LANE)), _resident((1, LANE)),
                  rope, rope],
        out_specs=[tok(ATTN_Q_WIDTH), tok(ATTN_KV_WIDTH), tok(ATTN_KV_HEADS * V_EXT)],
        out_shape=[jax.ShapeDtypeStruct((NB, SEQ, ATTN_Q_WIDTH), bf16),
                   jax.ShapeDtypeStruct((NB, SEQ, ATTN_KV_WIDTH), bf16),
                   jax.ShapeDtypeStruct((NB, SEQ, ATTN_KV_HEADS * V_EXT), bf16)],
        compiler_params=_cparams("parallel", "parallel"),
        name="attn_proj",
    )(h, mods, w, qn, kn, cos, sin)


ATTN_TQ = 512


def _latent_slabs_only(slab, o_ref, body):
    @pl.when(slab < BATCH)
    def _():
        body()

    @pl.when(slab >= BATCH)
    def _():
        o_ref[...] = jnp.zeros_like(o_ref)


def _attn_kernel(has_lat, q_ref, kc_ref, vc_ref, *rest):
    if has_lat:
        k_ref, v_ref, o_ref = rest
        _latent_slabs_only(pl.program_id(0), o_ref,
                           functools.partial(_attn_heads, q_ref, kc_ref, vc_ref, k_ref, v_ref, o_ref))
    else:
        (o_ref,) = rest
        _attn_heads(q_ref, kc_ref, vc_ref, None, None, o_ref)


def _attn_heads(q_ref, kc_ref, vc_ref, k_ref, v_ref, o_ref):
    has_lat = k_ref is not None
    for g in range(ATTN_KV_HEADS):
        cols = slice(g * LANE, (g + 1) * LANE)
        vcols = slice(g * V_EXT, (g + 1) * V_EXT)
        for hh in range(ATTN_GROUP):
            hcols = slice((g * ATTN_GROUP + hh) * LANE, (g * ATTN_GROUP + hh + 1) * LANE)
            q = q_ref[:, hcols]
            sc = _dot_nt(q, kc_ref[:, cols])
            m = jnp.max(sc, -1, keepdims=True)
            if has_lat:
                s = _dot_nt(q, k_ref[:, cols])
                m = jnp.maximum(m, jnp.max(s, -1, keepdims=True))
                ov = _dot(jnp.exp2(s - m).astype(bf16), v_ref[:, vcols])
                ov = ov + _dot(jnp.exp2(sc - m).astype(bf16), vc_ref[:, vcols])
            else:
                ov = _dot(jnp.exp2(sc - m).astype(bf16), vc_ref[:, vcols])
            o_ref[:, hcols] = (ov[:, :LANE] / ov[:, LANE:]).astype(bf16)


def _attention(q, k, v, latent, n_slabs=BATCH):
    tq = ATTN_TQ if latent else CTX_LEN
    vw = ATTN_KV_HEADS * V_EXT
    if latent:
        grid = (n_slabs, SEQ // tq)
        lat = lambda b: jnp.minimum(b, BATCH - 1)
        qspec = pl.BlockSpec((None, tq, ATTN_Q_WIDTH), lambda b, t: (lat(b), t, 0))
        ctx_kv = lambda width: pl.BlockSpec((None, CTX_LEN, width), lambda b, t: (BATCH, lat(b), 0))
        lat_kv = lambda width: pl.BlockSpec((None, SEQ, width), lambda b, t: (lat(b), 0, 0))
        in_specs = [qspec, ctx_kv(ATTN_KV_WIDTH), ctx_kv(vw), lat_kv(ATTN_KV_WIDTH), lat_kv(vw)]
        args = (q, k, v, k, v)
        out_shape = jax.ShapeDtypeStruct((n_slabs, SEQ, ATTN_Q_WIDTH), bf16)
    else:
        grid = (BATCH, 1)
        qspec = pl.BlockSpec((None, tq, ATTN_Q_WIDTH), lambda b, t: (BATCH, b, 0))
        ctx_kv = lambda width: pl.BlockSpec((None, CTX_LEN, width), lambda b, t: (BATCH, b, 0))
        in_specs = [qspec, ctx_kv(ATTN_KV_WIDTH), ctx_kv(vw)]
        args = (q, k, v)
        out_shape = jax.ShapeDtypeStruct((BATCH, CTX_LEN, ATTN_Q_WIDTH), bf16)
    return pl.pallas_call(
        functools.partial(_attn_kernel, latent),
        grid=grid,
        in_specs=in_specs,
        out_specs=pl.BlockSpec((None, tq, ATTN_Q_WIDTH), lambda b, t: (b, t, 0)),
        out_shape=out_shape,
        compiler_params=_cparams("parallel", "parallel"),
        name="attention",
    )(*args)


def _with_ctx_slab(lat, ctx):
    return lat.at[BATCH].set(ctx.reshape(SEQ, ctx.shape[-1]))


def _dft_tables(length):
    idx = np.arange(length, dtype=np.int64)
    ang = (np.outer(idx, idx) % (2 * length)).astype(np.float64) * (math.pi / length)
    return tuple(jnp.asarray(t, dtype=f32).astype(bf16) for t in (np.cos(ang), -np.sin(ang)))


def _filter_kernel(length, feats_ref, t_ref, dl_ref, w1_ref, b1_ref, fr1_ref, w2_ref, b2_ref, fr2_ref,
                   w3_ref, b3_ref, cos_ref, sin_ref, kre_ref, kim_ref):
    h = jnp.sin(fr1_ref[...] * (_dot_f32(feats_ref[...], w1_ref[...]) + b1_ref[...]))
    h = jnp.sin(fr2_ref[...] * (_dot_f32(h, w2_ref[...]) + b2_ref[...]))
    h = _dot_f32(h, w3_ref[...]) + b3_ref[...]
    win = jnp.exp(-t_ref[...] * dl_ref[...])
    row = lax.broadcasted_iota(jnp.int32, (length, HYENA_WIDTH), 0)
    hf = h[:, :HYENA_WIDTH] * win
    hb = jnp.where(row == 0, 0.0, h[:, HYENA_WIDTH:] * win)
    inv = lax.rsqrt(jnp.sum(hf * hf + hb * hb, 0, keepdims=True) + 1e-6)
    even = (hf + hb) * inv
    odd = (hf - hb) * inv
    sign = jnp.where((row & 1) == 0, 1.0, -1.0)
    nyq = jnp.sum(even * sign, 0, keepdims=True)
    wgt = jnp.where(row == 0, 1.0, 2.0) * (0.5 / length)
    e0, e1, _ = _split3(even)
    o0, o1, _ = _split3(odd)
    cm, sm = cos_ref[...], sin_ref[...]
    kre = _dot(cm, e0) + _dot(cm, e1)
    kim = _dot(sm, o0) + _dot(sm, o1)
    kre_ref[...] = kre * wgt
    kim_ref[...] = jnp.where(row == 0, nyq, kim) * wgt


def _hyena_filter(length, w1, b1, fr1, w2, b2, fr2, w3, b3, cos_m, sin_m):
    n = np.arange(length, dtype=np.float32)[:, None]
    t = np.linspace(0.0, 1.0, length, dtype=np.float32)[:, None]
    bands = np.linspace(1e-4, HYENA_BANDS - 1, HYENA_BANDS, dtype=np.float32)[None, :]
    ang = (np.float32(2.0 * math.pi) * n * bands / np.float32(length)).astype(np.float32)
    feats = np.zeros((length, EMB_PAD), np.float32)
    feats[:, :HYENA_EMB] = np.concatenate([t, np.cos(ang), -np.sin(ang)], axis=-1)
    max_decay = math.log(HYENA_TARGET) / HYENA_FAST_DECAY
    min_decay = math.log(HYENA_TARGET) / HYENA_SLOW_DECAY
    deltas = np.abs(np.linspace(min_decay, max_decay, HYENA_WIDTH, dtype=np.float32))[None, :]
    w1p = jnp.zeros((EMB_PAD, HYENA_FILTER_HIDDEN), f32).at[:HYENA_EMB].set(w1)
    row = lambda v: v.reshape(1, -1)
    args = (jnp.asarray(feats), jnp.asarray(t), jnp.asarray(deltas), w1p, row(b1), row(fr1), w2, row(b2), row(fr2),
            w3, row(b3), cos_m, sin_m)
    out = jax.ShapeDtypeStruct((length, HYENA_WIDTH), f32)
    return pl.pallas_call(
        functools.partial(_filter_kernel, length),
        out_shape=[out, out],
        compiler_params=pltpu.CompilerParams(vmem_limit_bytes=VMEM_LIMIT),
        name="hyena_filter",
    )(*args)


def _conv3(p_ref, w_ref, b_ref, length):
    x = p_ref[...]
    row = lax.broadcasted_iota(jnp.int32, x.shape, 0)
    prev = jnp.where(row == 0, 0.0, pltpu.roll(x, 1, 0))
    nxt = jnp.where(row == length - 1, 0.0, pltpu.roll(x, length - 1, 0))
    return prev * w_ref[0:1, :] + x * w_ref[1:2, :] + nxt * w_ref[2:3, :] + b_ref[...]


def _hyena_kernel(length, *refs):
    if length == SEQ:
        _latent_slabs_only(pl.program_id(1), refs[-3], functools.partial(_hyena_body, length, *refs))
    else:
        _hyena_body(length, *refs)


def _hyena_body(length, p0_ref, p1_ref, p2_ref, w0_ref, w1_ref, w2_ref, b0_ref, b1_ref, b2_ref, skip_ref,
                kre_ref, kim_ref, cos_ref, sin_ref, o_ref, yre_ref, yim_ref):
    z = _conv3(p0_ref, w0_ref, b0_ref, length) * _conv3(p1_ref, w1_ref, b1_ref, length)
    zb = z.astype(bf16)
    ft = min(length, HY_FT)
    for f0 in range(0, length, ft):
        rows = slice(f0, f0 + ft)
        zre = _dot(cos_ref[rows, :], zb)
        zim = _dot(sin_ref[rows, :], zb)
        kre, kim = kre_ref[rows, :], kim_ref[rows, :]
        yre_ref[rows, :] = (zre * kre - zim * kim).astype(bf16)
        yim_ref[rows, :] = (zre * kim + zim * kre).astype(bf16)
    row = lax.broadcasted_iota(jnp.int32, z.shape, 0)
    sign = jnp.where((row & 1) == 0, 1.0, -1.0)
    y_nyq = jnp.sum(z * sign, 0, keepdims=True) * kim_ref[0:1, :]
    y = _dot(cos_ref[...], yre_ref[...]) + _dot(sin_ref[...], yim_ref[...]) + sign * y_nyq
    o_ref[...] = ((y + z * skip_ref[...]) * _conv3(p2_ref, w2_ref, b2_ref, length)).astype(bf16)


def _hyena(p, length, conv_w, conv_b, skip, kre, kim, cos_m, sin_m):
    nblk = HYENA_WIDTH // HY_CBLK
    n_out = NB if length == SEQ else BATCH
    if length == SEQ:
        part = lambda j: pl.BlockSpec((None, length, HY_CBLK),
                                      lambda c, b: (jnp.minimum(b, BATCH - 1), 0, j * nblk + c))
    else:
        part = lambda j: pl.BlockSpec((None, length, HY_CBLK), lambda c, b: (BATCH, b, j * nblk + c))
    wpart = lambda j, rows: pl.BlockSpec((rows, HY_CBLK), lambda c, b: (0, j * nblk + c))
    chan = pl.BlockSpec((1, HY_CBLK), lambda c, b: (0, c))
    spec_k = pl.BlockSpec((length, HY_CBLK), lambda c, b: (0, c), pipeline_mode=pl.Buffered(1))
    table = _resident((length, length))
    return pl.pallas_call(
        functools.partial(_hyena_kernel, length),
        grid=(nblk, n_out),
        in_specs=[part(0), part(1), part(2), wpart(0, 3), wpart(1, 3), wpart(2, 3), wpart(0, 1), wpart(1, 1),
                  wpart(2, 1), chan, spec_k, spec_k, table, table],
        out_specs=pl.BlockSpec((None, length, HY_CBLK), lambda c, b: (b, 0, c)),
        out_shape=jax.ShapeDtypeStruct((n_out, length, HYENA_WIDTH), bf16),
        scratch_shapes=[pltpu.VMEM((length, HY_CBLK), bf16), pltpu.VMEM((length, HY_CBLK), bf16)],
        compiler_params=_cparams("parallel", "parallel"),
        name="hyena",
    )(p, p, p, conv_w, conv_w, conv_w, conv_b, conv_b, conv_b, skip, kre, kim, cos_m, sin_m)


GLA_GROUP_LAT = 4
GLA_GROUP_CTX = 2
GLA_HEADS_PER_TILE = LANE // GLA_DK
GLA_PRE_TILE = 256


def _log_sigmoid(x):
    return jnp.minimum(x, 0.0) - jnp.log(1.0 + jnp.exp(-jnp.abs(x)))


def _head_tile(hd):
    tile = hd // GLA_HEADS_PER_TILE
    lane = lax.broadcasted_iota(jnp.int32, (1, LANE), 1)
    first = (hd % GLA_HEADS_PER_TILE) * GLA_DK
    return slice(tile * LANE, (tile + 1) * LANE), (lane >= first) & (lane < first + GLA_DK)


def _aligned(x, m):
    return x if isinstance(x, int) else pl.multiple_of(x, m)


def _chunk_tri(n_rows, direction):
    ri = lax.broadcasted_iota(jnp.int32, (n_rows, n_rows), 0)
    ci = lax.broadcasted_iota(jnp.int32, (n_rows, n_rows), 1)
    same_chunk = (ri // GLA_CHUNK) == (ci // GLA_CHUNK)
    return same_chunk & ((ci <= ri) if direction == 0 else (ci >= ri))


def _gla_decay_tile(direction, refs, gup_ref, gb_ref, scr, rb, sb):
    q_ref, k_ref, _, _, lo_ref = refs
    qin_ref, kin_ref, kst_ref, dec_ref = scr
    group = GLA_PRE_TILE // GLA_CHUNK
    rows = pl.ds(rb, GLA_PRE_TILE)
    tri_b = jnp.where(_chunk_tri(GLA_PRE_TILE, direction), 1.0, 0.0).astype(bf16)
    x = _dot(lo_ref[rows, :].astype(bf16), gup_ref[direction]) + gb_ref[direction]
    lg = _log_sigmoid(x) * (1.0 / GLA_TAU)
    l0, l1, _ = _split3(lg)
    cum = _dot(tri_b, l0) + _dot(tri_b, l1)
    tots = []
    for g in range(group):
        end = g * GLA_CHUNK + (GLA_CHUNK - 1 if direction == 0 else 0)
        tots.append(cum[end:end + 1, :])
    tot_rows = jnp.concatenate([jnp.broadcast_to(t, (GLA_CHUNK, t.shape[1])) for t in tots], 0)
    k = k_ref[rows, :]
    srows = pl.ds(sb, GLA_PRE_TILE)
    qin_ref[direction, srows, :] = (q_ref[rows, :] * (GLA_DK ** -0.5) * jnp.exp(cum)).astype(bf16)
    kin_ref[direction, srows, :] = (k * jnp.exp(-cum)).astype(bf16)
    kst_ref[direction, srows, :] = (k * jnp.exp(tot_rows - cum)).astype(bf16)
    for g in range(group):
        drow = _aligned((sb // GLA_CHUNK + g) * SUBLANE, SUBLANE)
        dec_ref[direction, pl.ds(drow, SUBLANE), :] = jnp.broadcast_to(jnp.exp(tots[g]), (SUBLANE, GLA_QK))


def _gla_block_prep(direction, refs, scr, rb, sb, group):
    v_ref = refs[2]
    qin_ref, kin_ref, kst_ref, dec_ref = scr
    n_rows = group * GLA_CHUNK
    tri = _chunk_tri(n_rows, direction)
    srows = pl.ds(sb, n_rows)
    q_in, k_in, k_st = qin_ref[direction, srows, :], kin_ref[direction, srows, :], kst_ref[direction, srows, :]
    dec = dec_ref[direction, pl.ds(_aligned(sb // GLA_CHUNK * SUBLANE, group * SUBLANE), group * SUBLANE), :]
    decs = [dec[g * SUBLANE:g * SUBLANE + 1, :] for g in range(group)]
    v = v_ref[pl.ds(rb, n_rows), :].astype(bf16)
    q_heads, intra, ds_t = [], [], [[] for _ in range(group)]
    for hd in range(GLA_HEADS):
        cols, lanes = _head_tile(hd)
        vc = slice(hd * GLA_DV, (hd + 1) * GLA_DV)
        q_h = jnp.where(lanes, q_in[:, cols], 0.0).astype(bf16)
        kst_h = jnp.where(lanes, k_st[:, cols], 0.0).astype(bf16)
        att = jnp.where(tri, _dot_nt(q_h, k_in[:, cols]), 0.0)
        q_heads.append(q_h)
        intra.append(_dot(att.astype(bf16), v[:, vc]))
        for g in range(group):
            cr = slice(g * GLA_CHUNK, (g + 1) * GLA_CHUNK)
            ds_t[g].append(_dot_tn(v[cr, vc], kst_h[cr, :]))
    return q_heads, intra, ds_t, decs


def _gla_block_scan(direction, prep, st_ref, group):
    q_heads, intra, ds_t, decs = prep
    order = range(group) if direction == 0 else range(group - 1, -1, -1)
    outs = []
    for hd in range(GLA_HEADS):
        cols, _ = _head_tile(hd)
        st = st_ref[direction, hd]
        inter = [None] * group
        for g in order:
            cr = slice(g * GLA_CHUNK, (g + 1) * GLA_CHUNK)
            inter[g] = _dot_nt(q_heads[hd][cr, :], st.astype(bf16))
            st = st * decs[g][:, cols] + ds_t[g][hd]
        st_ref[direction, hd] = st
        outs.append(intra[hd] + jnp.concatenate(inter, 0))
    return jnp.concatenate(outs, axis=-1)


def _gla_finish(o, g, ng):
    outs = []
    for hd in range(GLA_HEADS):
        oh = o[:, hd * GLA_DV:(hd + 1) * GLA_DV]
        outs.append(oh * lax.rsqrt(jnp.mean(oh * oh, -1, keepdims=True) + RMS_EPS) * ng)
    return jnp.concatenate(outs, axis=-1) * (g * jax.nn.sigmoid(g))


def _gla_scan(refs, o_ref, acc_ref, scr, ng_ref, st_ref, n_chunks, group, s0):
    g_ref = refs[3]
    n_rows = group * GLA_CHUNK
    n_blocks = n_chunks // group
    half = n_blocks // 2

    def step(i, finish):
        bases = [pl.multiple_of(blk * n_rows, n_rows) for blk in (i, n_blocks - 1 - i)]
        preps = [_gla_block_prep(d, refs, scr, bases[d], _aligned(s0 + bases[d], n_rows), group) for d in (0, 1)]
        for d in (0, 1):
            rows = pl.ds(bases[d], n_rows)
            o = _gla_block_scan(d, preps[d], st_ref, group)
            if finish:
                o_ref[rows, :] = _gla_finish(acc_ref[rows, :] + o, g_ref[rows, :], ng_ref[...]).astype(bf16)
            else:
                acc_ref[rows, :] = o

    def first(i, carry):
        step(i, False)
        return carry

    def second(i, carry):
        step(i, True)
        return carry

    lax.fori_loop(0, half, first, 0)
    lax.fori_loop(half, n_blocks, second, 0)


def _gla_kernel(*refs):
    _latent_slabs_only(pl.program_id(0), refs[13], functools.partial(_gla_sequence, *refs))


def _gla_sequence(*refs):
    lat, ctx = refs[0:5], refs[5:10]
    gup_ref, gb_ref, ng_ref, o_ref, oc_ref, st_ref, acc_ref = refs[10:17]
    scr = refs[17:]
    st_ref[...] = jnp.zeros_like(st_ref)

    def decay_tiles(t, carry):
        rb = pl.multiple_of(t * GLA_PRE_TILE, GLA_PRE_TILE)
        for d in (0, 1):
            _gla_decay_tile(d, lat, gup_ref, gb_ref, scr, rb, rb)
        return carry

    for d in (0, 1):
        _gla_decay_tile(d, ctx, gup_ref, gb_ref, scr, 0, SEQ)
    lax.fori_loop(0, SEQ // GLA_PRE_TILE, decay_tiles, 0)
    _gla_scan(ctx, oc_ref, acc_ref, scr, ng_ref, st_ref, CTX_LEN // GLA_CHUNK, GLA_GROUP_CTX, SEQ)
    _gla_scan(lat, o_ref, acc_ref, scr, ng_ref, st_ref, SEQ // GLA_CHUNK, GLA_GROUP_LAT, 0)


def _gla(p, gup, gb, ng):
    seq = lambda b: jnp.minimum(b, BATCH - 1)

    def pieces(latent):
        ln = SEQ if latent else CTX_LEN
        if latent:
            blk = lambda width, off: pl.BlockSpec((None, ln, width), lambda b: (seq(b), 0, off // width))
        else:
            blk = lambda width, off: pl.BlockSpec((None, ln, width), lambda b: (BATCH, seq(b), off // width))
        return [blk(GLA_QK, EVEN_Q), blk(GLA_QK, EVEN_K), blk(GLA_WIDTH, EVEN_V), blk(GLA_WIDTH, EVEN_G),
                blk(LANE, EVEN_LO)]

    all_rows = SEQ + CTX_LEN
    return pl.pallas_call(
        _gla_kernel,
        grid=(NB,),
        in_specs=pieces(True) + pieces(False) + [_resident(gup.shape), _resident(gb.shape), _resident(ng.shape)],
        out_specs=[pl.BlockSpec((None, SEQ, GLA_WIDTH), lambda b: (b, 0, 0)),
                   pl.BlockSpec((None, CTX_LEN, GLA_WIDTH), lambda b: (seq(b), 0, 0))],
        out_shape=[jax.ShapeDtypeStruct((NB, SEQ, GLA_WIDTH), bf16),
                   jax.ShapeDtypeStruct((BATCH, CTX_LEN, GLA_WIDTH), bf16)],
        scratch_shapes=[pltpu.VMEM((2, GLA_HEADS, GLA_DV, LANE), f32), pltpu.VMEM((SEQ, GLA_WIDTH), f32),
                        pltpu.VMEM((2, all_rows, GLA_QK), bf16), pltpu.VMEM((2, all_rows, GLA_QK), bf16),
                        pltpu.VMEM((2, all_rows, GLA_QK), bf16),
                        pltpu.VMEM((2, all_rows // GLA_CHUNK * SUBLANE, GLA_QK), f32)],
        compiler_params=_cparams("arbitrary"),
        name="gla",
    )(*([p] * 10), gup, gb, ng)


def _even_w_in_aligned(w_in):
    main, lo = w_in[:, :EVEN_LO], w_in[:, EVEN_LO:]
    return jnp.concatenate([main, jnp.pad(lo, ((0, 0), (0, LANE - 2 * GLA_RANK)))], -1).astype(bf16)


def _gla_gate_params(gate_up, gate_b):
    ups = [jnp.zeros((LANE, GLA_QK), f32).at[j * GLA_RANK:(j + 1) * GLA_RANK].set(gate_up[j]) for j in range(2)]
    return jnp.stack(ups).astype(bf16), gate_b.reshape(2, 1, GLA_QK)


_HEAD_PERM = np.concatenate([np.arange(0, 32), np.arange(64, 96), np.arange(32, 64), np.arange(96, 128)])


def _attn_w_in_permuted(w_in):
    nqk = ATTN_Q_HEADS + ATTN_KV_HEADS
    cols = (np.arange(nqk)[:, None] * ATTN_HEAD_DIM + _HEAD_PERM[None, :]).reshape(-1)
    cols = np.concatenate([cols, np.arange(nqk * ATTN_HEAD_DIM, nqk * ATTN_HEAD_DIM + ATTN_KV_WIDTH)])
    return w_in[:, cols].astype(bf16)


def _rope_tables():
    rows = SEQ // GRID_W
    row = jnp.repeat(jnp.arange(rows, dtype=f32), GRID_W)
    col = jnp.tile(jnp.arange(GRID_W, dtype=f32), rows)
    inv_freq = ROPE_THETA ** (-jnp.arange(ROPE_AXIS_PAIRS, dtype=f32) / ROPE_AXIS_PAIRS)
    ang_r = row[:, None] * inv_freq
    ang_c = col[:, None] * inv_freq
    cos = jnp.concatenate([jnp.cos(ang_r), jnp.cos(ang_c), jnp.cos(ang_r), jnp.cos(ang_c)], -1)
    sin = jnp.concatenate([-jnp.sin(ang_r), -jnp.sin(ang_c), jnp.sin(ang_r), jnp.sin(ang_c)], -1)
    return cos, sin


def kernel(x, c, ctx, c_ctx, ada_w, ada_b, ln_g, ln_b, ffn_w1, ffn_w2, even_w_in, even_w_out, hyena_conv_w, hyena_conv_b, hyena_f_w1, hyena_f_b1, hyena_f_fr1, hyena_f_w2, hyena_f_b2, hyena_f_fr2, hyena_f_w3, hyena_f_b3, hyena_skip, gla_gate_up, gla_gate_b, gla_norm_g, attn_w_in, attn_w_out, attn_q_norm, attn_k_norm):
    c_all = jnp.concatenate([c, c_ctx[None], jnp.zeros((MOD_ROWS - NB, D_MODEL), f32)], 0)
    mods = _adaln(c_all, ada_w, ada_b).reshape(DEPTH, MOD_ROWS, N_MOD, D_MODEL)
    cos, sin = _rope_tables()
    dft_lat = _dft_tables(SEQ)
    dft_ctx = _dft_tables(CTX_LEN)
    w1, w2 = ffn_w1, ffn_w2
    lng = ln_g.reshape(DEPTH, 3, 1, D_MODEL)
    lnb = ln_b.reshape(DEPTH, 3, 1, D_MODEL)

    h = (x, ctx.reshape(1, SEQ, D_MODEL))
    for i in range(DEPTH):
        need_ctx = i < DEPTH - 1
        h = _ffn(h, mods, i, 0, 0, w1, w2, lng, lnb, NB)
        if i % 2 == 0:
            e = i // 2
            w_out, w_idx = even_w_out, e
            gup, gb = _gla_gate_params(gla_gate_up[e], gla_gate_b[e])
            p = _proj(h, mods, i, _even_w_in_aligned(even_w_in[e]))
            filt = (hyena_f_w1[e], hyena_f_b1[e], hyena_f_fr1[e], hyena_f_w2[e], hyena_f_b2[e], hyena_f_fr2[e],
                    hyena_f_w3[e], hyena_f_b3[e])
            conv_b = hyena_conv_b[e].reshape(1, HYENA_PROJ)
            skip = hyena_skip[e].reshape(1, HYENA_WIDTH)
            kre, kim = _hyena_filter(SEQ, *filt, dft_lat[0], dft_lat[1])
            yh = _hyena(p, SEQ, hyena_conv_w[e], conv_b, skip, kre, kim, *dft_lat)
            kre_c, kim_c = _hyena_filter(CTX_LEN, *filt, dft_ctx[0], dft_ctx[1])
            yhc = _hyena(p, CTX_LEN, hyena_conv_w[e], conv_b, skip, kre_c, kim_c, *dft_ctx)
            yg, ygc = _gla(p, gup, gb, gla_norm_g[e].reshape(1, GLA_DV))
            xs = [_with_ctx_slab(yh, yhc), _with_ctx_slab(yg, ygc)]
        else:
            o = i // 2
            w_out, w_idx = attn_w_out, o
            qn = attn_q_norm[o][_HEAD_PERM].reshape(1, LANE)
            kn = attn_k_norm[o][_HEAD_PERM].reshape(1, LANE)
            q, k, v = _attn_proj(h, mods, i, _attn_w_in_permuted(attn_w_in[o]), qn, kn, cos, sin)
            att = _attention(q, k, v, True, NB if need_ctx else BATCH)
            xs = [_with_ctx_slab(att, _attention(q, k, v, False)) if need_ctx else att]
        h = _mix_ffn(xs, w_out, w_idx, h, mods, i, w1, w2, lng, lnb, NB if need_ctx else BATCH)
    return h
```

```python
import functools
import math

import jax
import jax.numpy as jnp
import numpy as np
from jax import lax
from jax.experimental import pallas as pl
from jax.experimental.pallas import tpu as pltpu

f32 = jnp.float32
bf16 = jnp.bfloat16

D_MODEL = 1024
BATCH = 8
SEQ = 2048
DEPTH = 4
GRID_W = 64
CTX_LEN = 256
N_MOD = 9
DEEPNORM_ALPHA = (2.0 * DEPTH) ** 0.25
LN_EPS = 1e-5
RMS_EPS = 1e-6
MACARON_WEIGHT = 0.5
FFN_HIDDEN = 2816

HYENA_WIDTH = 512
HYENA_PROJ = 3 * HYENA_WIDTH
HYENA_BANDS = 16
HYENA_EMB = 1 + 2 * HYENA_BANDS
HYENA_FILTER_HIDDEN = 64
HYENA_FAST_DECAY = 0.3
HYENA_SLOW_DECAY = 1.5
HYENA_TARGET = 1e-2

GLA_WIDTH = 512
GLA_HEADS = 4
GLA_DV = 128
GLA_DK = 64
GLA_QK = GLA_HEADS * GLA_DK
GLA_RANK = 16
GLA_TAU = 16.0
GLA_CHUNK = 64

ATTN_HEAD_DIM = 128
ATTN_Q_HEADS = 8
ATTN_KV_HEADS = 2
ATTN_GROUP = 4
ATTN_Q_WIDTH = 1024
ATTN_KV_WIDTH = 256
ROPE_THETA = 10000.0
ROPE_AXIS_PAIRS = 32

assert BATCH * CTX_LEN == SEQ
NB = BATCH + 1

LANE = 128
SUBLANE = 8
MOD_ROWS = 16
EMB_PAD = 128
VMEM_LIMIT = 56 * 1024 * 1024
TOKEN_TILE = 512
PROJ_TILE = 1024

EVEN_Q = HYENA_PROJ
EVEN_K = EVEN_Q + GLA_QK
EVEN_V = EVEN_K + GLA_QK
EVEN_G = EVEN_V + GLA_WIDTH
EVEN_LO = EVEN_G + GLA_WIDTH
EVEN_W = EVEN_LO + LANE
HY_CBLK = 256
HY_FT = 512
V_EXT = 2 * LANE


def _cparams(*sem):
    return pltpu.CompilerParams(dimension_semantics=sem, vmem_limit_bytes=VMEM_LIMIT)


def _resident(shape, index=None):
    index = tuple(index or ())
    block = (None,) * len(index) + tuple(shape[len(index):])
    tail = (0,) * (len(shape) - len(index))
    return pl.BlockSpec(block, lambda *_: index + tail, pipeline_mode=pl.Buffered(1))


def _dot(a, b):
    return jnp.dot(a, b, preferred_element_type=f32)


def _dot_nt(a, b):
    return lax.dot_general(a, b, (((1,), (1,)), ((), ())), preferred_element_type=f32)


def _dot_tn(a, b):
    return lax.dot_general(a, b, (((0,), (0,)), ((), ())), preferred_element_type=f32)


def _split3(x):
    hi = x.astype(bf16)
    r1 = x - hi.astype(f32)
    mid = r1.astype(bf16)
    lo = (r1 - mid.astype(f32)).astype(bf16)
    return hi, mid, lo


def _dot_f32(a, b):
    a0, a1, a2 = _split3(a)
    b0, b1, b2 = _split3(b)
    return (_dot(a0, b0) + (_dot(a0, b1) + _dot(a1, b0))
            + (_dot(a0, b2) + _dot(a1, b1) + _dot(a2, b0)))


def _layer_norm(x, g, b):
    mu = jnp.mean(x, -1, keepdims=True)
    xc = x - mu
    var = jnp.mean(xc * xc, -1, keepdims=True)
    return xc * lax.rsqrt(var + LN_EPS) * g + b


def _mod_rows(mod_ref, slot):
    return (mod_ref[3 * slot:3 * slot + 1, :], mod_ref[3 * slot + 1:3 * slot + 2, :],
            mod_ref[3 * slot + 2:3 * slot + 3, :])


def _mod_spec(layer):
    return pl.BlockSpec((None, None, N_MOD, D_MODEL), lambda b, *_: (layer, b, 0, 0))


def _tok_spec(width):
    return pl.BlockSpec((None, TOKEN_TILE, width), lambda b, t: (b, t, 0))


def _ln_specs(layer, slot):
    return [_resident((DEPTH, 3, 1, D_MODEL), (layer, slot))] * 2


def _adaln_kernel(c_ref, w_ref, b_ref, o_ref):
    c = c_ref[...]
    sc = (c * jax.nn.sigmoid(c)).astype(bf16)
    o_ref[...] = _dot(sc, w_ref[...].astype(bf16)) + b_ref[...]


def _adaln(c_all, ada_w, ada_b):
    return pl.pallas_call(
        _adaln_kernel,
        grid=(DEPTH, N_MOD),
        in_specs=[
            pl.BlockSpec((MOD_ROWS, D_MODEL), lambda l, j: (0, 0)),
            pl.BlockSpec((None, D_MODEL, D_MODEL), lambda l, j: (l, 0, j)),
            pl.BlockSpec((None, 1, D_MODEL), lambda l, j: (l, 0, j)),
        ],
        out_specs=pl.BlockSpec((None, MOD_ROWS, D_MODEL), lambda l, j: (l, 0, j)),
        out_shape=jax.ShapeDtypeStruct((DEPTH, MOD_ROWS, N_MOD * D_MODEL), f32),
        compiler_params=_cparams("parallel", "parallel"),
        name="adaln",
    )(c_all, ada_w, ada_b.reshape(DEPTH, 1, N_MOD * D_MODEL))


FFN_TH = 256


def _ffn_sublayer(h, mod_ref, slot, w1_ref, w2_ref, g_ref, b_ref):
    shift, scale, gate = _mod_rows(mod_ref, slot)
    a = (h * (1.0 + scale) + shift).astype(bf16)
    acc = None
    for lo in range(0, FFN_HIDDEN, FFN_TH):
        gt = _dot(a, w1_ref[:, lo:lo + FFN_TH].astype(bf16))
        up = _dot(a, w1_ref[:, FFN_HIDDEN + lo:FFN_HIDDEN + lo + FFN_TH].astype(bf16))
        act = (gt * jax.nn.sigmoid(gt) * up).astype(bf16)
        part = _dot(act, w2_ref[lo:lo + FFN_TH, :].astype(bf16))
        acc = part if acc is None else acc + part
    return _layer_norm(DEEPNORM_ALPHA * h + gate * (MACARON_WEIGHT * acc), g_ref[...], b_ref[...])


def _ffn_kernel(slot, split_input, *refs):
    if split_input:
        x_ref, c_ref, mod_ref, w1_ref, w2_ref, g_ref, b_ref, o_ref = refs
        h = jnp.where(pl.program_id(0) == BATCH, c_ref[...], x_ref[...])
    else:
        h_ref, mod_ref, w1_ref, w2_ref, g_ref, b_ref, o_ref = refs
        h = h_ref[...]
    o_ref[...] = _ffn_sublayer(h, mod_ref, slot, w1_ref, w2_ref, g_ref, b_ref)


def _ffn(h, mods, layer, slot, which, w1, w2, ln_g, ln_b, n_slabs):
    split_input = isinstance(h, tuple)
    if split_input:
        tok_specs = [pl.BlockSpec((None, TOKEN_TILE, D_MODEL), lambda b, t: (jnp.minimum(b, BATCH - 1), t, 0)),
                     pl.BlockSpec((None, TOKEN_TILE, D_MODEL), lambda b, t: (0, t, 0))]
        hs = h
    else:
        tok_specs, hs = [_tok_spec(D_MODEL)], (h,)
    return pl.pallas_call(
        functools.partial(_ffn_kernel, slot, split_input),
        grid=(n_slabs, SEQ // TOKEN_TILE),
        in_specs=tok_specs + [_mod_spec(layer), _resident(w1.shape, (layer, which)),
                              _resident(w2.shape, (layer, which)), *_ln_specs(layer, slot)],
        out_specs=_tok_spec(D_MODEL),
        out_shape=jax.ShapeDtypeStruct((n_slabs, SEQ, D_MODEL), f32),
        compiler_params=_cparams("parallel", "parallel"),
        name="ffn",
    )(*hs, mods, w1, w2, ln_g, ln_b)


def _mix_ffn_kernel(n_in, *refs):
    x_refs, wo_refs = refs[:n_in], refs[n_in:2 * n_in]
    h_ref, mod_ref, g1_ref, b1_ref, w1_ref, w2_ref, g2_ref, b2_ref, o_ref = refs[2 * n_in:]
    _, _, gate = _mod_rows(mod_ref, 1)
    y = None
    for x_ref, wo_ref in zip(x_refs, wo_refs):
        part = _dot(x_ref[...], wo_ref[...].astype(bf16))
        y = part if y is None else y + part
    h1 = _layer_norm(DEEPNORM_ALPHA * h_ref[...] + gate * y, g1_ref[...], b1_ref[...])
    o_ref[...] = _ffn_sublayer(h1, mod_ref, 2, w1_ref, w2_ref, g2_ref, b2_ref)


def _mix_ffn(xs, w_out, w_idx, h, mods, layer, w1, w2, ln_g, ln_b, n_slabs):
    in_specs, row = [], 0
    for x in xs:
        in_specs.append(_tok_spec(x.shape[-1]))
    for x in xs:
        width = x.shape[-1]
        in_specs.append(pl.BlockSpec((None, width, D_MODEL), lambda b, t, r=row // width: (w_idx, r, 0),
                                     pipeline_mode=pl.Buffered(1)))
        row += width
    return pl.pallas_call(
        functools.partial(_mix_ffn_kernel, len(xs)),
        grid=(n_slabs, SEQ // TOKEN_TILE),
        in_specs=in_specs + [_tok_spec(D_MODEL), _mod_spec(layer), *_ln_specs(layer, 1),
                             _resident(w1.shape, (layer, 1)), _resident(w2.shape, (layer, 1)), *_ln_specs(layer, 2)],
        out_specs=_tok_spec(D_MODEL),
        out_shape=jax.ShapeDtypeStruct((n_slabs, SEQ, D_MODEL), f32),
        compiler_params=_cparams("parallel", "parallel"),
        name="mix_ffn",
    )(*xs, *([w_out] * len(xs)), h, mods, ln_g, ln_b, w1, w2, ln_g, ln_b)


def _proj_kernel(tn, h_ref, mod_ref, w_ref, o_ref):
    shift, scale, _ = _mod_rows(mod_ref, 1)
    a = (h_ref[...] * (1.0 + scale) + shift).astype(bf16)
    n = w_ref.shape[1]
    for j in range(0, n, tn):
        w = min(tn, n - j)
        o_ref[:, j:j + w] = _dot(a, w_ref[:, j:j + w])


def _proj(h, mods, layer, w):
    n = w.shape[1]
    tok = lambda width: pl.BlockSpec((None, PROJ_TILE, width), lambda b, t: (b, t, 0))
    return pl.pallas_call(
        functools.partial(_proj_kernel, 512),
        grid=(NB, SEQ // PROJ_TILE),
        in_specs=[tok(D_MODEL), _mod_spec(layer), _resident(w.shape)],
        out_specs=tok(n),
        out_shape=jax.ShapeDtypeStruct((NB, SEQ, n), f32),
        compiler_params=_cparams("parallel", "parallel"),
        name="even_proj",
    )(h, mods, w)


ATTN_PROJ_TILE = 256
Q_PRESCALE = ATTN_HEAD_DIM ** -0.5 * math.log2(math.e)


def _attn_proj_kernel(h_ref, mod_ref, w_ref, qn_ref, kn_ref, cos_ref, sin_ref, q_ref, k_ref, v_ref):
    shift, scale, _ = _mod_rows(mod_ref, 1)
    a = (h_ref[...] * (1.0 + scale) + shift).astype(bf16)
    is_ctx = pl.program_id(0) == BATCH
    cos = jnp.where(is_ctx, 1.0, cos_ref[...])
    sin = jnp.where(is_ctx, 0.0, sin_ref[...])

    def norm_rope(x, gain):
        x = x * lax.rsqrt(jnp.mean(x * x, -1, keepdims=True) + RMS_EPS) * gain
        return x * cos + pltpu.roll(x, ATTN_HEAD_DIM // 2, 1) * sin

    for half in range(2):
        x4 = _dot(a, w_ref[:, half * 512:(half + 1) * 512])
        for hh in range(4):
            hd = half * 4 + hh
            q_ref[:, hd * LANE:(hd + 1) * LANE] = (
                norm_rope(x4[:, hh * LANE:(hh + 1) * LANE], qn_ref[...]) * Q_PRESCALE).astype(bf16)
    kx = _dot(a, w_ref[:, ATTN_Q_WIDTH:ATTN_Q_WIDTH + ATTN_KV_WIDTH])
    for hd in range(ATTN_KV_HEADS):
        k_ref[:, hd * LANE:(hd + 1) * LANE] = norm_rope(kx[:, hd * LANE:(hd + 1) * LANE], kn_ref[...]).astype(bf16)
    vx = _dot(a, w_ref[:, ATTN_Q_WIDTH + ATTN_KV_WIDTH:ATTN_Q_WIDTH + 2 * ATTN_KV_WIDTH])
    ones = jnp.ones((a.shape[0], LANE), bf16)
    for hd in range(ATTN_KV_HEADS):
        v_ref[:, hd * V_EXT:hd * V_EXT + LANE] = vx[:, hd * LANE:(hd + 1) * LANE].astype(bf16)
        v_ref[:, hd * V_EXT + LANE:(hd + 1) * V_EXT] = ones


def _attn_proj(h, mods, layer, w, qn, kn, cos, sin):
    tm = ATTN_PROJ_TILE
    tok = lambda width: pl.BlockSpec((None, tm, width), lambda b, t: (b, t, 0))
    rope = pl.BlockSpec((tm, LANE), lambda b, t: (t, 0))
    return pl.pallas_call(
        _attn_proj_kernel,
        grid=(NB, SEQ // tm),
        in_specs=[tok(D_MODEL), _mod_spec(layer), _resident(w.shape), _resident((1, LANE)), _resident((1, LANE)),
                  rope, rope],
        out_specs=[tok(ATTN_Q_WIDTH), tok(ATTN_KV_WIDTH), tok(ATTN_KV_HEADS * V_EXT)],
        out_shape=[jax.ShapeDtypeStruct((NB, SEQ, ATTN_Q_WIDTH), bf16),
                   jax.ShapeDtypeStruct((NB, SEQ, ATTN_KV_WIDTH), bf16),
                   jax.ShapeDtypeStruct((NB, SEQ, ATTN_KV_HEADS * V_EXT), bf16)],
        compiler_params=_cparams("parallel", "parallel"),
        name="attn_proj",
    )(h, mods, w, qn, kn, cos, sin)


ATTN_TQ = 512


def _latent_slabs_only(slab, o_ref, body):
    @pl.when(slab < BATCH)
    def _():
        body()

    @pl.when(slab >= BATCH)
    def _():
        o_ref[...] = jnp.zeros_like(o_ref)


def _attn_kernel(has_lat, q_ref, kc_ref, vc_ref, *rest):
    if has_lat:
        k_ref, v_ref, kp_ref, vp_ref, o_ref = rest
        slab = pl.program_id(0)

        @pl.when(slab < BATCH)
        def _():
            _attn_heads(q_ref, kc_ref, vc_ref, k_ref, v_ref, o_ref)

        @pl.when(slab >= BATCH)
        def _():
            for s0 in range(0, q_ref.shape[0], CTX_LEN):
                rows = pl.ds(s0, CTX_LEN)
                _attn_heads(q_ref.at[rows], kp_ref.at[rows], vp_ref.at[rows], None, None, o_ref.at[rows])
    else:
        (o_ref,) = rest
        _attn_heads(q_ref, kc_ref, vc_ref, None, None, o_ref)


def _attn_heads(q_ref, kc_ref, vc_ref, k_ref, v_ref, o_ref):
    has_lat = k_ref is not None
    for g in range(ATTN_KV_HEADS):
        cols = slice(g * LANE, (g + 1) * LANE)
        vcols = slice(g * V_EXT, (g + 1) * V_EXT)
        for hh in range(ATTN_GROUP):
            hcols = slice((g * ATTN_GROUP + hh) * LANE, (g * ATTN_GROUP + hh + 1) * LANE)
            q = q_ref[:, hcols]
            sc = _dot_nt(q, kc_ref[:, cols])
            m = jnp.max(sc, -1, keepdims=True)
            if has_lat:
                s = _dot_nt(q, k_ref[:, cols])
                m = jnp.maximum(m, jnp.max(s, -1, keepdims=True))
                ov = _dot(jnp.exp2(s - m).astype(bf16), v_ref[:, vcols])
                ov = ov + _dot(jnp.exp2(sc - m).astype(bf16), vc_ref[:, vcols])
            else:
                ov = _dot(jnp.exp2(sc - m).astype(bf16), vc_ref[:, vcols])
            o_ref[:, hcols] = (ov[:, :LANE] / ov[:, LANE:]).astype(bf16)


def _attention(q, k, v, latent, n_slabs=BATCH):
    tq = ATTN_TQ if latent else CTX_LEN
    vw = ATTN_KV_HEADS * V_EXT
    if latent:
        grid = (n_slabs, SEQ // tq)
        lat = lambda b: jnp.minimum(b, BATCH - 1)
        qspec = pl.BlockSpec((None, tq, ATTN_Q_WIDTH), lambda b, t: (b, t, 0))
        ctx_kv = lambda width: pl.BlockSpec((None, CTX_LEN, width), lambda b, t: (BATCH, lat(b), 0))
        lat_kv = lambda width: pl.BlockSpec((None, SEQ, width), lambda b, t: (lat(b), 0, 0))
        own_kv = lambda width: pl.BlockSpec((None, tq, width), lambda b, t: (BATCH, jnp.where(b == BATCH, t, 0), 0))
        in_specs = [qspec, ctx_kv(ATTN_KV_WIDTH), ctx_kv(vw), lat_kv(ATTN_KV_WIDTH), lat_kv(vw),
                    own_kv(ATTN_KV_WIDTH), own_kv(vw)]
        args = (q, k, v, k, v, k, v)
        out_shape = jax.ShapeDtypeStruct((n_slabs, SEQ, ATTN_Q_WIDTH), bf16)
    else:
        grid = (BATCH, 1)
        qspec = pl.BlockSpec((None, tq, ATTN_Q_WIDTH), lambda b, t: (BATCH, b, 0))
        ctx_kv = lambda width: pl.BlockSpec((None, CTX_LEN, width), lambda b, t: (BATCH, b, 0))
        in_specs = [qspec, ctx_kv(ATTN_KV_WIDTH), ctx_kv(vw)]
        args = (q, k, v)
        out_shape = jax.ShapeDtypeStruct((BATCH, CTX_LEN, ATTN_Q_WIDTH), bf16)
    return pl.pallas_call(
        functools.partial(_attn_kernel, latent),
        grid=grid,
        in_specs=in_specs,
        out_specs=pl.BlockSpec((None, tq, ATTN_Q_WIDTH), lambda b, t: (b, t, 0)),
        out_shape=out_shape,
        compiler_params=_cparams("parallel", "parallel"),
        name="attention",
    )(*args)


def _with_ctx_slab(lat, ctx):
    return lat.at[BATCH].set(ctx.reshape(SEQ, ctx.shape[-1]))


def _dft_tables(length):
    idx = np.arange(length, dtype=np.int64)
    ang = (np.outer(idx, idx) % (2 * length)).astype(np.float64) * (math.pi / length)
    return tuple(jnp.asarray(t, dtype=f32).astype(bf16) for t in (np.cos(ang), -np.sin(ang)))


def _filter_kernel(length, feats_ref, t_ref, dl_ref, w1_ref, b1_ref, fr1_ref, w2_ref, b2_ref, fr2_ref,
                   w3_ref, b3_ref, cos_ref, sin_ref, kre_ref, kim_ref):
    h = jnp.sin(fr1_ref[...] * (_dot_f32(feats_ref[...], w1_ref[...]) + b1_ref[...]))
    h = jnp.sin(fr2_ref[...] * (_dot_f32(h, w2_ref[...]) + b2_ref[...]))
    h = _dot_f32(h, w3_ref[...]) + b3_ref[...]
    win = jnp.exp(-t_ref[...] * dl_ref[...])
    row = lax.broadcasted_iota(jnp.int32, (length, HYENA_WIDTH), 0)
    hf = h[:, :HYENA_WIDTH] * win
    hb = jnp.where(row == 0, 0.0, h[:, HYENA_WIDTH:] * win)
    inv = lax.rsqrt(jnp.sum(hf * hf + hb * hb, 0, keepdims=True) + 1e-6)
    even = (hf + hb) * inv
    odd = (hf - hb) * inv
    sign = jnp.where((row & 1) == 0, 1.0, -1.0)
    nyq = jnp.sum(even * sign, 0, keepdims=True)
    wgt = jnp.where(row == 0, 1.0, 2.0) * (0.5 / length)
    e0, e1, _ = _split3(even)
    o0, o1, _ = _split3(odd)
    cm, sm = cos_ref[...], sin_ref[...]
    kre = _dot(cm, e0) + _dot(cm, e1)
    kim = _dot(sm, o0) + _dot(sm, o1)
    kre_ref[...] = kre * wgt
    kim_ref[...] = jnp.where(row == 0, nyq, kim) * wgt


def _hyena_filter(length, w1, b1, fr1, w2, b2, fr2, w3, b3, cos_m, sin_m):
    n = np.arange(length, dtype=np.float32)[:, None]
    t = np.linspace(0.0, 1.0, length, dtype=np.float32)[:, None]
    bands = np.linspace(1e-4, HYENA_BANDS - 1, HYENA_BANDS, dtype=np.float32)[None, :]
    ang = (np.float32(2.0 * math.pi) * n * bands / np.float32(length)).astype(np.float32)
    feats = np.zeros((length, EMB_PAD), np.float32)
    feats[:, :HYENA_EMB] = np.concatenate([t, np.cos(ang), -np.sin(ang)], axis=-1)
    max_decay = math.log(HYENA_TARGET) / HYENA_FAST_DECAY
    min_decay = math.log(HYENA_TARGET) / HYENA_SLOW_DECAY
    deltas = np.abs(np.linspace(min_decay, max_decay, HYENA_WIDTH, dtype=np.float32))[None, :]
    w1p = jnp.zeros((EMB_PAD, HYENA_FILTER_HIDDEN), f32).at[:HYENA_EMB].set(w1)
    row = lambda v: v.reshape(1, -1)
    args = (jnp.asarray(feats), jnp.asarray(t), jnp.asarray(deltas), w1p, row(b1), row(fr1), w2, row(b2), row(fr2),
            w3, row(b3), cos_m, sin_m)
    out = jax.ShapeDtypeStruct((length, HYENA_WIDTH), f32)
    return pl.pallas_call(
        functools.partial(_filter_kernel, length),
        out_shape=[out, out],
        compiler_params=pltpu.CompilerParams(vmem_limit_bytes=VMEM_LIMIT),
        name="hyena_filter",
    )(*args)


def _conv3(p_ref, w_ref, b_ref, length):
    x = p_ref[...]
    row = lax.broadcasted_iota(jnp.int32, x.shape, 0)
    prev = jnp.where(row == 0, 0.0, pltpu.roll(x, 1, 0))
    nxt = jnp.where(row == length - 1, 0.0, pltpu.roll(x, length - 1, 0))
    return prev * w_ref[0:1, :] + x * w_ref[1:2, :] + nxt * w_ref[2:3, :] + b_ref[...]


def _hyena_kernel(length, *refs):
    if length == SEQ:
        _latent_slabs_only(pl.program_id(1), refs[-3], functools.partial(_hyena_body, length, *refs))
    else:
        _hyena_body(length, *refs)


def _hyena_body(length, p0_ref, p1_ref, p2_ref, w0_ref, w1_ref, w2_ref, b0_ref, b1_ref, b2_ref, skip_ref,
                kre_ref, kim_ref, cos_ref, sin_ref, o_ref, yre_ref, yim_ref):
    z = _conv3(p0_ref, w0_ref, b0_ref, length) * _conv3(p1_ref, w1_ref, b1_ref, length)
    zb = z.astype(bf16)
    ft = min(length, HY_FT)
    for f0 in range(0, length, ft):
        rows = slice(f0, f0 + ft)
        zre = _dot(cos_ref[rows, :], zb)
        zim = _dot(sin_ref[rows, :], zb)
        kre, kim = kre_ref[rows, :], kim_ref[rows, :]
        yre_ref[rows, :] = (zre * kre - zim * kim).astype(bf16)
        yim_ref[rows, :] = (zre * kim + zim * kre).astype(bf16)
    row = lax.broadcasted_iota(jnp.int32, z.shape, 0)
    sign = jnp.where((row & 1) == 0, 1.0, -1.0)
    y_nyq = jnp.sum(z * sign, 0, keepdims=True) * kim_ref[0:1, :]
    y = _dot(cos_ref[...], yre_ref[...]) + _dot(sin_ref[...], yim_ref[...]) + sign * y_nyq
    o_ref[...] = ((y + z * skip_ref[...]) * _conv3(p2_ref, w2_ref, b2_ref, length)).astype(bf16)


def _hyena(p, length, conv_w, conv_b, skip, kre, kim, cos_m, sin_m):
    nblk = HYENA_WIDTH // HY_CBLK
    n_out = NB if length == SEQ else BATCH
    if length == SEQ:
        part = lambda j: pl.BlockSpec((None, length, HY_CBLK),
                                      lambda c, b: (jnp.minimum(b, BATCH - 1), 0, j * nblk + c))
    else:
        part = lambda j: pl.BlockSpec((None, length, HY_CBLK), lambda c, b: (BATCH, b, j * nblk + c))
    wpart = lambda j, rows: pl.BlockSpec((rows, HY_CBLK), lambda c, b: (0, j * nblk + c))
    chan = pl.BlockSpec((1, HY_CBLK), lambda c, b: (0, c))
    spec_k = pl.BlockSpec((length, HY_CBLK), lambda c, b: (0, c), pipeline_mode=pl.Buffered(1))
    table = _resident((length, length))
    return pl.pallas_call(
        functools.partial(_hyena_kernel, length),
        grid=(nblk, n_out),
        in_specs=[part(0), part(1), part(2), wpart(0, 3), wpart(1, 3), wpart(2, 3), wpart(0, 1), wpart(1, 1),
                  wpart(2, 1), chan, spec_k, spec_k, table, table],
        out_specs=pl.BlockSpec((None, length, HY_CBLK), lambda c, b: (b, 0, c)),
        out_shape=jax.ShapeDtypeStruct((n_out, length, HYENA_WIDTH), bf16),
        scratch_shapes=[pltpu.VMEM((length, HY_CBLK), bf16), pltpu.VMEM((length, HY_CBLK), bf16)],
        compiler_params=_cparams("parallel", "parallel"),
        name="hyena",
    )(p, p, p, conv_w, conv_w, conv_w, conv_b, conv_b, conv_b, skip, kre, kim, cos_m, sin_m)


GLA_GROUP_LAT = 4
GLA_GROUP_CTX = 2
GLA_HEADS_PER_TILE = LANE // GLA_DK
GLA_PRE_TILE = 256


def _log_sigmoid(x):
    return jnp.minimum(x, 0.0) - jnp.log(1.0 + jnp.exp(-jnp.abs(x)))


def _head_tile(hd):
    tile = hd // GLA_HEADS_PER_TILE
    lane = lax.broadcasted_iota(jnp.int32, (1, LANE), 1)
    first = (hd % GLA_HEADS_PER_TILE) * GLA_DK
    return slice(tile * LANE, (tile + 1) * LANE), (lane >= first) & (lane < first + GLA_DK)


def _aligned(x, m):
    return x if isinstance(x, int) else pl.multiple_of(x, m)


def _chunk_tri(n_rows, direction):
    ri = lax.broadcasted_iota(jnp.int32, (n_rows, n_rows), 0)
    ci = lax.broadcasted_iota(jnp.int32, (n_rows, n_rows), 1)
    same_chunk = (ri // GLA_CHUNK) == (ci // GLA_CHUNK)
    return same_chunk & ((ci <= ri) if direction == 0 else (ci >= ri))


def _gla_decay_tile(direction, refs, gup_ref, gb_ref, scr, rb, sb):
    q_ref, k_ref, _, _, lo_ref = refs
    qin_ref, kin_ref, kst_ref, dec_ref = scr
    group = GLA_PRE_TILE // GLA_CHUNK
    rows = pl.ds(rb, GLA_PRE_TILE)
    tri_b = jnp.where(_chunk_tri(GLA_PRE_TILE, direction), 1.0, 0.0).astype(bf16)
    x = _dot(lo_ref[rows, :].astype(bf16), gup_ref[direction]) + gb_ref[direction]
    lg = _log_sigmoid(x) * (1.0 / GLA_TAU)
    l0, l1, _ = _split3(lg)
    cum = _dot(tri_b, l0) + _dot(tri_b, l1)
    tots = []
    for g in range(group):
        end = g * GLA_CHUNK + (GLA_CHUNK - 1 if direction == 0 else 0)
        tots.append(cum[end:end + 1, :])
    tot_rows = jnp.concatenate([jnp.broadcast_to(t, (GLA_CHUNK, t.shape[1])) for t in tots], 0)
    k = k_ref[rows, :]
    srows = pl.ds(sb, GLA_PRE_TILE)
    qin_ref[direction, srows, :] = (q_ref[rows, :] * (GLA_DK ** -0.5) * jnp.exp(cum)).astype(bf16)
    kin_ref[direction, srows, :] = (k * jnp.exp(-cum)).astype(bf16)
    kst_ref[direction, srows, :] = (k * jnp.exp(tot_rows - cum)).astype(bf16)
    for g in range(group):
        drow = _aligned((sb // GLA_CHUNK + g) * SUBLANE, SUBLANE)
        dec_ref[direction, pl.ds(drow, SUBLANE), :] = jnp.broadcast_to(jnp.exp(tots[g]), (SUBLANE, GLA_QK))


def _gla_block_prep(direction, refs, scr, rb, sb, group):
    v_ref = refs[2]
    qin_ref, kin_ref, kst_ref, dec_ref = scr
    n_rows = group * GLA_CHUNK
    tri = _chunk_tri(n_rows, direction)
    srows = pl.ds(sb, n_rows)
    q_in, k_in, k_st = qin_ref[direction, srows, :], kin_ref[direction, srows, :], kst_ref[direction, srows, :]
    dec = dec_ref[direction, pl.ds(_aligned(sb // GLA_CHUNK * SUBLANE, group * SUBLANE), group * SUBLANE), :]
    decs = [dec[g * SUBLANE:g * SUBLANE + 1, :] for g in range(group)]
    v = v_ref[pl.ds(rb, n_rows), :].astype(bf16)
    q_heads, intra, ds_t = [], [], [[] for _ in range(group)]
    for hd in range(GLA_HEADS):
        cols, lanes = _head_tile(hd)
        vc = slice(hd * GLA_DV, (hd + 1) * GLA_DV)
        q_h = jnp.where(lanes, q_in[:, cols], 0.0).astype(bf16)
        kst_h = jnp.where(lanes, k_st[:, cols], 0.0).astype(bf16)
        att = jnp.where(tri, _dot_nt(q_h, k_in[:, cols]), 0.0)
        q_heads.append(q_h)
        intra.append(_dot(att.astype(bf16), v[:, vc]))
        for g in range(group):
            cr = slice(g * GLA_CHUNK, (g + 1) * GLA_CHUNK)
            ds_t[g].append(_dot_tn(v[cr, vc], kst_h[cr, :]))
    return q_heads, intra, ds_t, decs


def _gla_block_scan(direction, prep, st_ref, group):
    q_heads, intra, ds_t, decs = prep
    order = range(group) if direction == 0 else range(group - 1, -1, -1)
    outs = []
    for hd in range(GLA_HEADS):
        cols, _ = _head_tile(hd)
        st = st_ref[direction, hd]
        inter = [None] * group
        for g in order:
            cr = slice(g * GLA_CHUNK, (g + 1) * GLA_CHUNK)
            inter[g] = _dot_nt(q_heads[hd][cr, :], st.astype(bf16))
            st = st * decs[g][:, cols] + ds_t[g][hd]
        st_ref[direction, hd] = st
        outs.append(intra[hd] + jnp.concatenate(inter, 0))
    return jnp.concatenate(outs, axis=-1)


def _gla_finish(o, g, ng):
    outs = []
    for hd in range(GLA_HEADS):
        oh = o[:, hd * GLA_DV:(hd + 1) * GLA_DV]
        outs.append(oh * lax.rsqrt(jnp.mean(oh * oh, -1, keepdims=True) + RMS_EPS) * ng)
    return jnp.concatenate(outs, axis=-1) * (g * jax.nn.sigmoid(g))


def _gla_scan(refs, o_ref, acc_ref, scr, ng_ref, st_ref, n_chunks, group, s0):
    g_ref = refs[3]
    n_rows = group * GLA_CHUNK
    n_blocks = n_chunks // group
    half = n_blocks // 2

    def step(i, finish):
        bases = [pl.multiple_of(blk * n_rows, n_rows) for blk in (i, n_blocks - 1 - i)]
        preps = [_gla_block_prep(d, refs, scr, bases[d], _aligned(s0 + bases[d], n_rows), group) for d in (0, 1)]
        for d in (0, 1):
            rows = pl.ds(bases[d], n_rows)
            o = _gla_block_scan(d, preps[d], st_ref, group)
            if finish:
                o_ref[rows, :] = _gla_finish(acc_ref[rows, :] + o, g_ref[rows, :], ng_ref[...]).astype(bf16)
            else:
                acc_ref[rows, :] = o

    def first(i, carry):
        step(i, False)
        return carry

    def second(i, carry):
        step(i, True)
        return carry

    lax.fori_loop(0, half, first, 0)
    lax.fori_loop(half, n_blocks, second, 0)


def _gla_kernel(*refs):
    _latent_slabs_only(pl.program_id(0), refs[13], functools.partial(_gla_sequence, *refs))


def _gla_sequence(*refs):
    lat, ctx = refs[0:5], refs[5:10]
    gup_ref, gb_ref, ng_ref, o_ref, oc_ref, st_ref, acc_ref = refs[10:17]
    scr = refs[17:]
    st_ref[...] = jnp.zeros_like(st_ref)

    def decay_tiles(t, carry):
        rb = pl.multiple_of(t * GLA_PRE_TILE, GLA_PRE_TILE)
        for d in (0, 1):
            _gla_decay_tile(d, lat, gup_ref, gb_ref, scr, rb, rb)
        return carry

    for d in (0, 1):
        _gla_decay_tile(d, ctx, gup_ref, gb_ref, scr, 0, SEQ)
    lax.fori_loop(0, SEQ // GLA_PRE_TILE, decay_tiles, 0)
    _gla_scan(ctx, oc_ref, acc_ref, scr, ng_ref, st_ref, CTX_LEN // GLA_CHUNK, GLA_GROUP_CTX, SEQ)
    _gla_scan(lat, o_ref, acc_ref, scr, ng_ref, st_ref, SEQ // GLA_CHUNK, GLA_GROUP_LAT, 0)


def _gla(p, gup, gb, ng):
    seq = lambda b: jnp.minimum(b, BATCH - 1)

    def pieces(latent):
        ln = SEQ if latent else CTX_LEN
        if latent:
            blk = lambda width, off: pl.BlockSpec((None, ln, width), lambda b: (seq(b), 0, off // width))
        else:
            blk = lambda width, off: pl.BlockSpec((None, ln, width), lambda b: (BATCH, seq(b), off // width))
        return [blk(GLA_QK, EVEN_Q), blk(GLA_QK, EVEN_K), blk(GLA_WIDTH, EVEN_V), blk(GLA_WIDTH, EVEN_G),
                blk(LANE, EVEN_LO)]

    all_rows = SEQ + CTX_LEN
    return pl.pallas_call(
        _gla_kernel,
        grid=(NB,),
        in_specs=pieces(True) + pieces(False) + [_resident(gup.shape), _resident(gb.shape), _resident(ng.shape)],
        out_specs=[pl.BlockSpec((None, SEQ, GLA_WIDTH), lambda b: (b, 0, 0)),
                   pl.BlockSpec((None, CTX_LEN, GLA_WIDTH), lambda b: (seq(b), 0, 0))],
        out_shape=[jax.ShapeDtypeStruct((NB, SEQ, GLA_WIDTH), bf16),
                   jax.ShapeDtypeStruct((BATCH, CTX_LEN, GLA_WIDTH), bf16)],
        scratch_shapes=[pltpu.VMEM((2, GLA_HEADS, GLA_DV, LANE), f32), pltpu.VMEM((SEQ, GLA_WIDTH), f32),
                        pltpu.VMEM((2, all_rows, GLA_QK), bf16), pltpu.VMEM((2, all_rows, GLA_QK), bf16),
                        pltpu.VMEM((2, all_rows, GLA_QK), bf16),
                        pltpu.VMEM((2, all_rows // GLA_CHUNK * SUBLANE, GLA_QK), f32)],
        compiler_params=_cparams("arbitrary"),
        name="gla",
    )(*([p] * 10), gup, gb, ng)


def _even_w_in_aligned(w_in):
    main, lo = w_in[:, :EVEN_LO], w_in[:, EVEN_LO:]
    return jnp.concatenate([main, jnp.pad(lo, ((0, 0), (0, LANE - 2 * GLA_RANK)))], -1).astype(bf16)


def _gla_gate_params(gate_up, gate_b):
    ups = [jnp.zeros((LANE, GLA_QK), f32).at[j * GLA_RANK:(j + 1) * GLA_RANK].set(gate_up[j]) for j in range(2)]
    return jnp.stack(ups).astype(bf16), gate_b.reshape(2, 1, GLA_QK)


_HEAD_PERM = np.concatenate([np.arange(0, 32), np.arange(64, 96), np.arange(32, 64), np.arange(96, 128)])


def _attn_w_in_permuted(w_in):
    nqk = ATTN_Q_HEADS + ATTN_KV_HEADS
    cols = (np.arange(nqk)[:, None] * ATTN_HEAD_DIM + _HEAD_PERM[None, :]).reshape(-1)
    cols = np.concatenate([cols, np.arange(nqk * ATTN_HEAD_DIM, nqk * ATTN_HEAD_DIM + ATTN_KV_WIDTH)])
    return w_in[:, cols].astype(bf16)


def _rope_tables():
    rows = SEQ // GRID_W
    row = jnp.repeat(jnp.arange(rows, dtype=f32), GRID_W)
    col = jnp.tile(jnp.arange(GRID_W, dtype=f32), rows)
    inv_freq = ROPE_THETA ** (-jnp.arange(ROPE_AXIS_PAIRS, dtype=f32) / ROPE_AXIS_PAIRS)
    ang_r = row[:, None] * inv_freq
    ang_c = col[:, None] * inv_freq
    cos = jnp.concatenate([jnp.cos(ang_r), jnp.cos(ang_c), jnp.cos(ang_r), jnp.cos(ang_c)], -1)
    sin = jnp.concatenate([-jnp.sin(ang_r), -jnp.sin(ang_c), jnp.sin(ang_r), jnp.sin(ang_c)], -1)
    return cos, sin


def kernel(x, c, ctx, c_ctx, ada_w, ada_b, ln_g, ln_b, ffn_w1, ffn_w2, even_w_in, even_w_out, hyena_conv_w, hyena_conv_b, hyena_f_w1, hyena_f_b1, hyena_f_fr1, hyena_f_w2, hyena_f_b2, hyena_f_fr2, hyena_f_w3, hyena_f_b3, hyena_skip, gla_gate_up, gla_gate_b, gla_norm_g, attn_w_in, attn_w_out, attn_q_norm, attn_k_norm):
    c_all = jnp.concatenate([c, c_ctx[None], jnp.zeros((MOD_ROWS - NB, D_MODEL), f32)], 0)
    mods = _adaln(c_all, ada_w, ada_b).reshape(DEPTH, MOD_ROWS, N_MOD, D_MODEL)
    cos, sin = _rope_tables()
    dft_lat = _dft_tables(SEQ)
    dft_ctx = _dft_tables(CTX_LEN)
    w1, w2 = ffn_w1, ffn_w2
    lng = ln_g.reshape(DEPTH, 3, 1, D_MODEL)
    lnb = ln_b.reshape(DEPTH, 3, 1, D_MODEL)

    h = (x, ctx.reshape(1, SEQ, D_MODEL))
    for i in range(DEPTH):
        need_ctx = i < DEPTH - 1
        h = _ffn(h, mods, i, 0, 0, w1, w2, lng, lnb, NB)
        if i % 2 == 0:
            e = i // 2
            w_out, w_idx = even_w_out, e
            gup, gb = _gla_gate_params(gla_gate_up[e], gla_gate_b[e])
            p = _proj(h, mods, i, _even_w_in_aligned(even_w_in[e]))
            filt = (hyena_f_w1[e], hyena_f_b1[e], hyena_f_fr1[e], hyena_f_w2[e], hyena_f_b2[e], hyena_f_fr2[e],
                    hyena_f_w3[e], hyena_f_b3[e])
            conv_b = hyena_conv_b[e].reshape(1, HYENA_PROJ)
            skip = hyena_skip[e].reshape(1, HYENA_WIDTH)
            kre, kim = _hyena_filter(SEQ, *filt, dft_lat[0], dft_lat[1])
            yh = _hyena(p, SEQ, hyena_conv_w[e], conv_b, skip, kre, kim, *dft_lat)
            kre_c, kim_c = _hyena_filter(CTX_LEN, *filt, dft_ctx[0], dft_ctx[1])
            yhc = _hyena(p, CTX_LEN, hyena_conv_w[e], conv_b, skip, kre_c, kim_c, *dft_ctx)
            yg, ygc = _gla(p, gup, gb, gla_norm_g[e].reshape(1, GLA_DV))
            xs = [_with_ctx_slab(yh, yhc), _with_ctx_slab(yg, ygc)]
        else:
            o = i // 2
            w_out, w_idx = attn_w_out, o
            qn = attn_q_norm[o][_HEAD_PERM].reshape(1, LANE)
            kn = attn_k_norm[o][_HEAD_PERM].reshape(1, LANE)
            q, k, v = _attn_proj(h, mods, i, _attn_w_in_permuted(attn_w_in[o]), qn, kn, cos, sin)
            xs = [_attention(q, k, v, True, NB if need_ctx else BATCH)]
        h = _mix_ffn(xs, w_out, w_idx, h, mods, i, w1, w2, lng, lnb, NB if need_ctx else BATCH)
    return h
```
